```python
import jax, jax.numpy as jnp
from jax import lax
import numpy as np

D_MODEL = 2048
BATCH = 2
SEQ = 16384
DEPTH = 4
DEC_BATCH = 2
DEC_SEQ = 8192
PAST_LEN = 128

GRID_W = 64
ATTN_WIDTH = D_MODEL // 2
CONV_WIDTH = D_MODEL - ATTN_WIDTH
N_HEADS = 8
HEAD_DIM = ATTN_WIDTH // N_HEADS
CONV_GROUPS = 8
CONV_GROUP_DIM = CONV_WIDTH // CONV_GROUPS
CONV_K = 3
WIN_ROWS = 8
WIN_COLS = 16
COL_BLOCK = 16
COL_BAND = 32
D_FF = -(-8 * D_MODEL // (3 * 256)) * 256
IN_WIDTH = 3 * ATTN_WIDTH + 3 * CONV_WIDTH
EPS = 1e-6
NEG_INF = -1e30

kernel_name = "hybrid_natten_shortconv_encoder"


def rms_norm(x, g):
    xf = x.astype(jnp.float32)
    y = xf * lax.rsqrt(jnp.mean(xf * xf, axis=-1, keepdims=True) + EPS)
    return (y * g.astype(jnp.float32)).astype(x.dtype)


def group_rms_norm(x, g, n_groups):
    shp = x.shape
    xg = x.reshape(shp[:-1] + (n_groups, shp[-1] // n_groups)).astype(jnp.float32)
    y = xg * lax.rsqrt(jnp.mean(xg * xg, axis=-1, keepdims=True) + EPS)
    return (y.reshape(shp) * g.astype(jnp.float32)).astype(x.dtype)


def short_conv(u, w):
    pad = (CONV_K - 1) // 2
    L = u.shape[1]
    up = jnp.pad(u, ((0, 0), (pad, CONV_K - 1 - pad), (0, 0)))
    out = up[:, 0:L] * w[0]
    for j in range(1, CONV_K):
        out = out + up[:, j:j + L] * w[j]
    return out


def neighbourhood_attention(q, k, v, rpb):
    b, L, h, dh = q.shape
    rows = L // GRID_W
    kr = min(WIN_ROWS, rows)
    n_cb = GRID_W // COL_BLOCK
    qg = q.reshape(b, rows, GRID_W, h, dh)
    kg = k.reshape(b, rows, GRID_W, h, dh)
    vg = v.reshape(b, rows, GRID_W, h, dh)

    qcol = np.arange(GRID_W)
    col_start = np.clip(qcol - WIN_COLS // 2, 0, GRID_W - WIN_COLS).reshape(n_cb, COL_BLOCK)
    band_start = np.clip(np.arange(n_cb) * COL_BLOCK - (COL_BAND - COL_BLOCK) // 2,
                         0, GRID_W - COL_BAND)
    band_cols = band_start[:, None] + np.arange(COL_BAND)
    qcols_blk = qcol.reshape(n_cb, COL_BLOCK)
    kc = band_cols[:, None, :]
    cs = col_start[:, :, None]
    col_valid = (kc >= cs) & (kc < cs + WIN_COLS)
    rel_c_idx = np.clip(kc - qcols_blk[:, :, None] + WIN_COLS - 1, 0, 2 * WIN_COLS - 2)
    col_mask = jnp.where(jnp.asarray(col_valid), 0.0, NEG_INF).astype(jnp.float32)
    rpb_c = rpb.astype(jnp.float32)[:, :, rel_c_idx] + col_mask
    scale = HEAD_DIM ** -0.5

    def row_step(r):
        r0 = jnp.clip(r - WIN_ROWS // 2, 0, rows - kr)
        q_r = lax.dynamic_index_in_dim(qg, r, axis=1, keepdims=False)
        k_r = lax.dynamic_slice_in_dim(kg, r0, kr, axis=1)
        v_r = lax.dynamic_slice_in_dim(vg, r0, kr, axis=1)
        k_band = k_r[:, :, band_cols]
        v_band = v_r[:, :, band_cols]
        q_blk = q_r.reshape(b, n_cb, COL_BLOCK, h, dh)
        s = jnp.einsum('bnqhd,brnkhd->bhnqrk', q_blk, k_band,
                       preferred_element_type=jnp.float32) * scale
        rel_r = r0 + jnp.arange(kr) - r + WIN_ROWS - 1
        bias = jnp.take(rpb_c, rel_r, axis=1)
        s = s + bias.transpose(0, 2, 3, 1, 4)[None]
        p = jax.nn.softmax(s.reshape(b, h, n_cb, COL_BLOCK, kr * COL_BAND), axis=-1)
        p = p.reshape(s.shape).astype(v.dtype)
        o = jnp.einsum('bhnqrk,brnkhd->bnqhd', p, v_band)
        return o.reshape(b, GRID_W, h, dh)

    out = lax.map(row_step, jnp.arange(rows))
    return out.transpose(1, 0, 2, 3, 4).reshape(b, L, h, dh)


def encoder_layer(x, norm1, w_in, q_gain, k_gain, rpb, conv_w, attn_out_gain,
                  conv_out_gain, w_out, norm2, w_gate, w_up, w_down):
    b, L, _ = x.shape
    hn = rms_norm(x, norm1)
    proj = hn @ w_in
    a = ATTN_WIDTH
    c = CONV_WIDTH
    q = proj[..., 0:a].reshape(b, L, N_HEADS, HEAD_DIM)
    k = proj[..., a:2 * a].reshape(b, L, N_HEADS, HEAD_DIM)
    v = proj[..., 2 * a:3 * a].reshape(b, L, N_HEADS, HEAD_DIM)
    gate_b = proj[..., 3 * a:3 * a + c]
    gate_c = proj[..., 3 * a + c:3 * a + 2 * c]
    u = proj[..., 3 * a + 2 * c:3 * a + 3 * c]
    q = rms_norm(q, q_gain)
    k = rms_norm(k, k_gain)
    attn = neighbourhood_attention(q, k, v, rpb).reshape(b, L, a)
    attn = group_rms_norm(attn, attn_out_gain, N_HEADS)
    conv = gate_b * short_conv(gate_c * u, conv_w)
    conv = group_rms_norm(conv, conv_out_gain, CONV_GROUPS)
    x = x + jnp.concatenate([attn, conv], axis=-1) @ w_out
    hn2 = rms_norm(x, norm2)
    x = x + (jax.nn.silu(hn2 @ w_gate) * (hn2 @ w_up)) @ w_down
    return x


def setup_inputs(seed: int = 0) -> dict:
    key = jax.random.key(seed)
    ks = jax.random.split(key, 16)
    f32 = jnp.float32
    nrm = lambda k, shape, s: jax.random.normal(k, shape, f32) * s
    return {
        "x_prompt": nrm(ks[0], (BATCH, SEQ, D_MODEL), 1.0),
        "x_sample": nrm(ks[1], (DEC_BATCH, DEC_SEQ, D_MODEL), 1.0),
        "norm1": 1.0 + nrm(ks[2], (DEPTH, D_MODEL), 0.05),
        "w_in": nrm(ks[3], (DEPTH, D_MODEL, IN_WIDTH), D_MODEL ** -0.5),
        "q_gain": 1.0 + nrm(ks[4], (DEPTH, HEAD_DIM), 0.05),
        "k_gain": 1.0 + nrm(ks[5], (DEPTH, HEAD_DIM), 0.05),
        "rpb": nrm(ks[6], (DEPTH, N_HEADS, 2 * WIN_ROWS - 1, 2 * WIN_COLS - 1), 0.5),
        "conv_w": nrm(ks[7], (DEPTH, CONV_K, CONV_WIDTH), CONV_K ** -0.5),
        "attn_out_gain": 1.0 + nrm(ks[8], (DEPTH, ATTN_WIDTH), 0.05),
        "conv_out_gain": 1.0 + nrm(ks[9], (DEPTH, CONV_WIDTH), 0.05),
        "w_out": nrm(ks[10], (DEPTH, D_MODEL, D_MODEL), D_MODEL ** -0.5),
        "norm2": 1.0 + nrm(ks[11], (DEPTH, D_MODEL), 0.05),
        "w_gate": nrm(ks[12], (DEPTH, D_MODEL, D_FF), D_MODEL ** -0.5),
        "w_up": nrm(ks[13], (DEPTH, D_MODEL, D_FF), D_MODEL ** -0.5),
        "w_down": nrm(ks[14], (DEPTH, D_FF, D_MODEL), D_FF ** -0.5),
    }


def reference(x_prompt, x_sample, norm1, w_in, q_gain, k_gain, rpb, conv_w,
              attn_out_gain, conv_out_gain, w_out, norm2, w_gate, w_up, w_down):
    y_prompt = x_prompt
    y_sample = x_sample
    for i in range(DEPTH):
        y_prompt = encoder_layer(y_prompt, norm1[i], w_in[i], q_gain[i], k_gain[i], rpb[i],
                                 conv_w[i], attn_out_gain[i], conv_out_gain[i], w_out[i],
                                 norm2[i], w_gate[i], w_up[i], w_down[i])
        y_sample = encoder_layer(y_sample, norm1[i], w_in[i], q_gain[i], k_gain[i], rpb[i],
                                 conv_w[i], attn_out_gain[i], conv_out_gain[i], w_out[i],
                                 norm2[i], w_gate[i], w_up[i], w_down[i])
    return (y_prompt, y_sample)
```

```python
import functools

import numpy as np
import jax
import jax.numpy as jnp
from jax import lax
from jax.experimental import pallas as pl
from jax.experimental.pallas import tpu as pltpu

GRID_W = 64
N_HEADS = 8
HEAD_DIM = 128
CONV_GROUPS = 8
CONV_K = 3
WIN_ROWS = 8
WIN_COLS = 16
EPS = 1e-6
NEG_INF = -1e30

LANES = 128
SUBLANES = 8
VMEM_LIMIT = 56 * 1024 * 1024

ROW_GROUP = 8
KEY_ROWS = ROW_GROUP + WIN_ROWS
N_REL_ROWS = 2 * WIN_ROWS - 1
N_REL_COLS = 2 * WIN_COLS - 1

F32 = jnp.float32
BF16 = jnp.bfloat16


def _params(*sem):
    return pltpu.CompilerParams(dimension_semantics=sem, vmem_limit_bytes=VMEM_LIMIT)


def _rms(x, gain):
    ms = jnp.mean(x * x, axis=-1, keepdims=True)
    return (x * lax.rsqrt(ms + EPS)) * gain


def _group_rms(x, gain, group):
    parts = []
    for g in range(x.shape[-1] // group):
        sl = slice(g * group, (g + 1) * group)
        parts.append(_rms(x[:, sl], gain[:, sl]))
    return parts


def _inproj_kernel(x_ref, g_ref, wa_ref, wc_ref, qg_ref, kg_ref, qkv_ref, bcu_ref, hn_ref,
                   *, n_q_blocks, n_k_blocks):
    j = pl.program_id(1)

    @pl.when(j == 0)
    def _():
        hn_ref[...] = _rms(x_ref[...], g_ref[...]).astype(BF16)

    hn = hn_ref[...]
    bcu_ref[...] = jnp.dot(hn, wc_ref[...], preferred_element_type=F32)
    acc = jnp.dot(hn, wa_ref[...], preferred_element_type=F32)
    n_heads_blk = acc.shape[-1] // HEAD_DIM

    def head_norm(gain):
        for hh in range(n_heads_blk):
            sl = slice(hh * HEAD_DIM, (hh + 1) * HEAD_DIM)
            qkv_ref[:, sl] = _rms(acc[:, sl], gain).astype(BF16)

    @pl.when(j < n_q_blocks)
    def _():
        head_norm(qg_ref[...])

    @pl.when(jnp.logical_and(j >= n_q_blocks, j < n_q_blocks + n_k_blocks))
    def _():
        head_norm(kg_ref[...])

    @pl.when(j >= n_q_blocks + n_k_blocks)
    def _():
        qkv_ref[...] = acc.astype(BF16)


def _in_projection(x, norm1, w_in, q_gain, k_gain, layer, *, tm, tn):
    m, d = x.shape
    attn_w = N_HEADS * HEAD_DIM
    half = w_in.shape[-1] // 2
    assert half == 3 * attn_w and half % tn == 0 and attn_w % tn == 0 and m % tm == 0
    nj = half // tn
    kern = functools.partial(_inproj_kernel, n_q_blocks=attn_w // tn, n_k_blocks=attn_w // tn)
    return pl.pallas_call(
        kern,
        out_shape=(jax.ShapeDtypeStruct((m, half), BF16), jax.ShapeDtypeStruct((m, half), F32)),
        grid=(m // tm, nj),
        in_specs=[
            pl.BlockSpec((tm, d), lambda i, j: (i, 0)),
            pl.BlockSpec((None, 1, d), lambda i, j: (layer, 0, 0)),
            pl.BlockSpec((None, d, tn), lambda i, j: (layer, 0, j)),
            pl.BlockSpec((None, d, tn), lambda i, j: (layer, 0, j + nj)),
            pl.BlockSpec((None, 1, HEAD_DIM), lambda i, j: (layer, 0, 0)),
            pl.BlockSpec((None, 1, HEAD_DIM), lambda i, j: (layer, 0, 0)),
        ],
        out_specs=(pl.BlockSpec((tm, tn), lambda i, j: (i, j)),
                   pl.BlockSpec((tm, tn), lambda i, j: (i, j))),
        scratch_shapes=[pltpu.VMEM((tm, d), BF16)],
        compiler_params=_params("parallel", "arbitrary"),
        name="in_projection",
    )(x, norm1, w_in, w_in, q_gain, k_gain)


def _row_group_variants(n_rows):
    variants = []
    for kind in range(3):
        table = []
        for i in range(ROW_GROUP):
            row = []
            for j in range(KEY_ROWS):
                if kind == 0:
                    q_abs, k_abs = i, j
                elif kind == 1:
                    q_abs, k_abs = KEY_ROWS + i, KEY_ROWS - WIN_ROWS // 2 + j
                else:
                    q_abs, k_abs = n_rows - ROW_GROUP + i, n_rows - KEY_ROWS + j
                r0 = q_abs - WIN_ROWS // 2
                if kind != 1:
                    r0 = min(max(r0, 0), n_rows - WIN_ROWS)
                valid = r0 <= k_abs < r0 + WIN_ROWS
                row.append(k_abs - q_abs + WIN_ROWS - 1 if valid else None)
            table.append(row)
        variants.append(table)
    return variants


def _attn_kernel(q_ref, k_ref, v_ref, t_ref, g_ref, o_ref, bias_ref, *, n_rows):
    n_groups = n_rows // ROW_GROUP
    gq = ROW_GROUP * GRID_W
    gk = KEY_ROWS * GRID_W
    scale = HEAD_DIM ** -0.5

    left = lax.broadcasted_iota(jnp.int32, (GRID_W, 2 * GRID_W), 1) < GRID_W
    neg = jnp.full((GRID_W, 2 * GRID_W), NEG_INF, F32)
    for kind, table in enumerate(_row_group_variants(n_rows)):
        for i in range(ROW_GROUP):
            for jj in range(KEY_ROWS // 2):
                dl, dr = table[i][2 * jj], table[i][2 * jj + 1]
                if dl is None and dr is None:
                    blk = neg
                else:
                    lo = neg if dl is None else t_ref[dl]
                    hi = neg if dr is None else t_ref[dr]
                    blk = jnp.where(left, lo, hi)
                bias_ref[kind, i * GRID_W:(i + 1) * GRID_W,
                         jj * 2 * GRID_W:(jj + 1) * 2 * GRID_W] = blk

    gain = g_ref[...]

    def body(g, carry):
        key_row0 = jnp.clip(g * ROW_GROUP - WIN_ROWS // 2, 0, n_rows - KEY_ROWS)
        kind = jnp.where(g == 0, 0, jnp.where(g == n_groups - 1, 2, 1))
        q0 = pl.multiple_of(g * gq, gq)
        k0 = pl.multiple_of(key_row0 * GRID_W, GRID_W)
        q = q_ref[pl.ds(q0, gq), :]
        k = k_ref[pl.ds(k0, gk), :]
        v = v_ref[pl.ds(k0, gk), :]
        s = lax.dot_general(q, k, (((1,), (1,)), ((), ())), preferred_element_type=F32)
        s = s * scale + bias_ref[kind]
        m = jnp.max(s, axis=-1, keepdims=True)
        p = jnp.exp(s - m)
        l = jnp.sum(p, axis=-1, keepdims=True)
        o = jnp.dot(p.astype(BF16), v, preferred_element_type=F32) / l
        o_ref[pl.ds(q0, gq), :] = _rms(o, gain).astype(o_ref.dtype)
        return carry

    lax.fori_loop(0, n_groups, body, 0)


def _attention(qkv, bias_blocks, out_gain, layer, *, n_rows):
    b, seq, _ = qkv.shape
    assert seq == n_rows * GRID_W and n_rows % ROW_GROUP == 0 and n_rows >= KEY_ROWS + ROW_GROUP
    blk = lambda off: pl.BlockSpec((None, seq, HEAD_DIM), lambda h, bi: (bi, 0, off + h))
    return pl.pallas_call(
        functools.partial(_attn_kernel, n_rows=n_rows),
        out_shape=jax.ShapeDtypeStruct((b, seq, N_HEADS * HEAD_DIM), BF16),
        grid=(N_HEADS, b),
        in_specs=[
            blk(0), blk(N_HEADS), blk(2 * N_HEADS),
            pl.BlockSpec((None, N_REL_ROWS, GRID_W, 2 * GRID_W), lambda h, bi: (h, 0, 0, 0)),
            pl.BlockSpec((None, 1, HEAD_DIM), lambda h, bi: (layer, 0, h)),
        ],
        out_specs=pl.BlockSpec((None, seq, HEAD_DIM), lambda h, bi: (bi, 0, h)),
        scratch_shapes=[pltpu.VMEM((3, ROW_GROUP * GRID_W, KEY_ROWS * GRID_W), F32)],
        compiler_params=_params("parallel", "parallel"),
        name="neighbourhood_attention",
    )(qkv, qkv, qkv, bias_blocks, out_gain)


def _bias_blocks(rpb_layer):
    c = np.arange(GRID_W)[:, None]
    kc = np.arange(GRID_W)[None, :]
    start = np.clip(c - WIN_COLS // 2, 0, GRID_W - WIN_COLS)
    valid = (kc >= start) & (kc < start + WIN_COLS)
    rel = np.clip(kc - c + WIN_COLS - 1, 0, N_REL_COLS - 1)
    t = jnp.where(jnp.asarray(valid), rpb_layer.astype(F32)[:, :, rel], NEG_INF)
    return jnp.concatenate([t, t], axis=-1)


def _conv_kernel(b_ref, c_ref, u_ref, cp_ref, up_ref, cn_ref, un_ref, w_ref, g_ref, o_ref):
    i = pl.program_id(1)
    last = pl.num_programs(1) - 1
    tl = c_ref.shape[0]
    cu = c_ref[...] * u_ref[...]
    prev_row = cp_ref[SUBLANES - 1:SUBLANES, :] * up_ref[SUBLANES - 1:SUBLANES, :]
    next_row = cn_ref[0:1, :] * un_ref[0:1, :]
    prev_row = jnp.where(i == 0, 0.0, prev_row)
    next_row = jnp.where(i == last, 0.0, next_row)
    rows = lax.broadcasted_iota(jnp.int32, cu.shape, 0)
    before = jnp.where(rows == 0, prev_row, pltpu.roll(cu, 1, 0))
    after = jnp.where(rows == tl - 1, next_row, pltpu.roll(cu, tl - 1, 0))
    w = w_ref[...]
    y = b_ref[...] * (before * w[0:1, :] + cu * w[1:2, :] + after * w[2:3, :])
    group = y.shape[-1] // CONV_GROUPS
    for g, part in enumerate(_group_rms(y, g_ref[...], group)):
        o_ref[:, g * group:(g + 1) * group] = part.astype(o_ref.dtype)


def _conv_branch(bcu, conv_w, out_gain, layer, *, tl):
    b, seq, width3 = bcu.shape
    cw = width3 // 3
    assert seq % tl == 0 and tl % SUBLANES == 0
    per = tl // SUBLANES
    n_halo = seq // SUBLANES
    main = lambda part: pl.BlockSpec((None, tl, cw), lambda bi, i: (bi, i, part))
    prev = lambda part: pl.BlockSpec(
        (None, SUBLANES, cw), lambda bi, i: (bi, jnp.maximum(i * per - 1, 0), part))
    nxt = lambda part: pl.BlockSpec(
        (None, SUBLANES, cw), lambda bi, i: (bi, jnp.minimum((i + 1) * per, n_halo - 1), part))
    return pl.pallas_call(
        _conv_kernel,
        out_shape=jax.ShapeDtypeStruct((b, seq, cw), BF16),
        grid=(b, seq // tl),
        in_specs=[
            main(0), main(1), main(2), prev(1), prev(2), nxt(1), nxt(2),
            pl.BlockSpec((None, CONV_K, cw), lambda bi, i: (layer, 0, 0)),
            pl.BlockSpec((None, 1, cw), lambda bi, i: (layer, 0, 0)),
        ],
        out_specs=pl.BlockSpec((None, tl, cw), lambda bi, i: (bi, i, 0)),
        compiler_params=_params("parallel", "parallel"),
        name="short_conv",
    )(bcu, bcu, bcu, bcu, bcu, bcu, bcu, conv_w, out_gain)


def _outproj_kernel(x_ref, a_ref, c_ref, wa_ref, wc_ref, o_ref):
    acc = jnp.dot(a_ref[...], wa_ref[...], preferred_element_type=F32)
    acc = acc + jnp.dot(c_ref[...], wc_ref[...], preferred_element_type=F32)
    o_ref[...] = x_ref[...] + acc


def _out_projection(x, attn, conv, w_out, layer, *, tm, tn):
    m, d = x.shape
    aw, cw = attn.shape[-1], conv.shape[-1]
    assert aw == cw and aw + cw == w_out.shape[1] and m % tm == 0 and d % tn == 0
    return pl.pallas_call(
        _outproj_kernel,
        out_shape=jax.ShapeDtypeStruct((m, d), F32),
        grid=(m // tm, d // tn),
        in_specs=[
            pl.BlockSpec((tm, tn), lambda i, j: (i, j)),
            pl.BlockSpec((tm, aw), lambda i, j: (i, 0)),
            pl.BlockSpec((tm, cw), lambda i, j: (i, 0)),
            pl.BlockSpec((None, aw, tn), lambda i, j: (layer, 0, j)),
            pl.BlockSpec((None, cw, tn), lambda i, j: (layer, 1, j)),
        ],
        out_specs=pl.BlockSpec((tm, tn), lambda i, j: (i, j)),
        compiler_params=_params("parallel", "parallel"),
        name="out_projection",
    )(x, attn, conv, w_out, w_out)


def _ffn_kernel(x_ref, g_ref, wg_ref, wu_ref, wd_ref, o_ref, hn_ref):
    j = pl.program_id(1)

    @pl.when(j == 0)
    def _():
        x = x_ref[...]
        hn_ref[...] = _rms(x, g_ref[...]).astype(BF16)
        o_ref[...] = x

    hn = hn_ref[...]
    gate = jnp.dot(hn, wg_ref[...], preferred_element_type=F32)
    up = jnp.dot(hn, wu_ref[...], preferred_element_type=F32)
    h = (gate * jax.nn.sigmoid(gate)) * up
    o_ref[...] += jnp.dot(h.astype(BF16), wd_ref[...], preferred_element_type=F32)


def _feed_forward(x, norm2, w_gate, w_up, w_down, layer, *, tm, tf):
    m, d = x.shape
    f = w_gate.shape[-1]
    assert m % tm == 0 and f % tf == 0
    return pl.pallas_call(
        _ffn_kernel,
        out_shape=jax.ShapeDtypeStruct((m, d), F32),
        grid=(m // tm, f // tf),
        in_specs=[
            pl.BlockSpec((tm, d), lambda i, j: (i, 0)),
            pl.BlockSpec((None, 1, d), lambda i, j: (layer, 0, 0)),
            pl.BlockSpec((None, d, tf), lambda i, j: (layer, 0, j)),
            pl.BlockSpec((None, d, tf), lambda i, j: (layer, 0, j)),
            pl.BlockSpec((None, tf, d), lambda i, j: (layer, j, 0)),
        ],
        out_specs=pl.BlockSpec((tm, d), lambda i, j: (i, 0)),
        scratch_shapes=[pltpu.VMEM((tm, d), BF16)],
        compiler_params=_params("parallel", "arbitrary"),
        name="feed_forward",
    )(x, norm2, w_gate, w_up, w_down)


def kernel(x_prompt, x_sample, norm1, w_in, q_gain, k_gain, rpb, conv_w, attn_out_gain,
           conv_out_gain, w_out, norm2, w_gate, w_up, w_down):
    depth, d, _ = w_in.shape
    w_in_b, w_gate_b, w_up_b, w_down_b = (w.astype(BF16) for w in (w_in, w_gate, w_up, w_down))
    w_out_b = w_out.astype(BF16)
    row = lambda a: a.reshape(depth, 1, a.shape[-1])
    norm1_r, norm2_r, qg_r, kg_r = row(norm1), row(norm2), row(q_gain), row(k_gain)
    ag_r, cg_r = row(attn_out_gain), row(conv_out_gain)

    streams = []
    for x in (x_prompt, x_sample):
        b, seq, _ = x.shape
        streams.append([x.reshape(b * seq, d), b, seq])

    for layer in range(depth):
        bias_blocks = _bias_blocks(rpb[layer])
        for st in streams:
            x, b, seq = st
            qkv, bcu = _in_projection(x, norm1_r, w_in_b, qg_r, kg_r, layer, tm=1024, tn=512)
            attn = _attention(qkv.reshape(b, seq, -1), bias_blocks, ag_r, layer,
                              n_rows=seq // GRID_W)
            conv = _conv_branch(bcu.reshape(b, seq, -1), conv_w, cg_r, layer, tl=512)
            x = _out_projection(x, attn.reshape(b * seq, -1), conv.reshape(b * seq, -1),
                                w_out_b, layer, tm=1024, tn=1024)
            st[0] = _feed_forward(x, norm2_r, w_gate_b, w_up_b, w_down_b, layer, tm=512, tf=512)

    return tuple(x.reshape(b, seq, d) for x, b, seq in streams)
```

```python
import functools
import math

import numpy as np
import jax
import jax.numpy as jnp
from jax import lax
from jax.experimental import pallas as pl
from jax.experimental.pallas import tpu as pltpu

GRID_W = 64
N_HEADS = 8
HEAD_DIM = 128
CONV_GROUPS = 8
CONV_K = 3
WIN_ROWS = 8
WIN_COLS = 16
EPS = 1e-6
NEG_INF = -1e30
LOG2_E = math.log2(math.e)

BF16_ROWS = 16
VMEM_LIMIT = 56 * 1024 * 1024

ROW_GROUP = 4
KEY_ROWS = ROW_GROUP + WIN_ROWS
N_REL_ROWS = 2 * WIN_ROWS - 1
N_REL_COLS = 2 * WIN_COLS - 1

F32 = jnp.float32
BF16 = jnp.bfloat16


def _params(*sem):
    return pltpu.CompilerParams(dimension_semantics=sem, vmem_limit_bytes=VMEM_LIMIT)


def _rms(x, gain):
    ms = jnp.mean(x * x, axis=-1, keepdims=True)
    return (x * lax.rsqrt(ms + EPS)) * gain


def _inproj_kernel(x_ref, xp_ref, xn_ref, g_ref, wa_ref, wb_ref, wc_ref, wu_ref, ag_ref, af_ref,
                   cw_ref, cg_ref, qkv_ref, conv_ref, hn_ref, *, tiles_per_seq):
    i = pl.program_id(0)
    j = pl.program_id(1)
    tm = x_ref.shape[0]
    halo = xp_ref.shape[0]

    @pl.when(j == 0)
    def _():
        g = g_ref[...]
        first = i % tiles_per_seq == 0
        last = i % tiles_per_seq == tiles_per_seq - 1
        hn_ref[0:halo, :] = jnp.where(first, 0.0, _rms(xp_ref[...], g)).astype(BF16)
        hn_ref[halo:halo + tm, :] = _rms(x_ref[...], g).astype(BF16)
        hn_ref[halo + tm:, :] = jnp.where(last, 0.0, _rms(xn_ref[...], g)).astype(BF16)

    hn_all = hn_ref[...]
    hn = hn_ref[halo:halo + tm, :]

    acc = jnp.dot(hn, wa_ref[...], preferred_element_type=F32)
    gain = ag_ref[...]
    normed = af_ref[...] > 0.0
    for hh in range(acc.shape[-1] // HEAD_DIM):
        sl = slice(hh * HEAD_DIM, (hh + 1) * HEAD_DIM)
        a = acc[:, sl]
        qkv_ref[:, sl] = jnp.where(normed[:, sl], _rms(a, gain[:, sl]), a).astype(BF16)

    gate_b = jnp.dot(hn, wb_ref[...], preferred_element_type=F32)
    cu = (jnp.dot(hn_all, wc_ref[...], preferred_element_type=F32)
          * jnp.dot(hn_all, wu_ref[...], preferred_element_type=F32))
    rows = cu.shape[0]
    before = pltpu.roll(cu, 1, 0)[halo:halo + tm]
    after = pltpu.roll(cu, rows - 1, 0)[halo:halo + tm]
    w = cw_ref[...]
    y = gate_b * (before * w[0:1, :] + cu[halo:halo + tm] * w[1:2, :] + after * w[2:3, :])
    cgain = cg_ref[...]
    group = HEAD_DIM
    for gi in range(y.shape[-1] // group):
        sl = slice(gi * group, (gi + 1) * group)
        conv_ref[:, sl] = _rms(y[:, sl], cgain[:, sl]).astype(BF16)


def _in_projection(x, norm1, w_in, qkv_gain, qkv_normed, conv_w, conv_gain, layer, *,
                   seq, tm, tn_qkv, tn_conv):
    m, d = x.shape
    attn_w = N_HEADS * HEAD_DIM
    conv_width = conv_w.shape[-1]
    assert w_in.shape[-1] == 3 * attn_w + 3 * conv_width
    assert conv_width // CONV_GROUPS == HEAD_DIM
    nj = 3 * attn_w // tn_qkv
    assert nj * tn_qkv == 3 * attn_w and nj * tn_conv == conv_width
    assert seq % tm == 0 and m % seq == 0 and tm % BF16_ROWS == 0
    halo = BF16_ROWS
    per = tm // halo
    n_halo = m // halo
    off_b = 3 * attn_w // tn_conv
    off_c = off_b + conv_width // tn_conv
    off_u = off_c + conv_width // tn_conv
    wspec = lambda width, off: pl.BlockSpec((None, d, width), lambda i, j: (layer, 0, off + j))
    rowspec = lambda width: pl.BlockSpec((None, 1, width), lambda i, j: (layer, 0, j))
    return pl.pallas_call(
        functools.partial(_inproj_kernel, tiles_per_seq=seq // tm),
        out_shape=(jax.ShapeDtypeStruct((m, 3 * attn_w), BF16),
                   jax.ShapeDtypeStruct((m, conv_width), BF16)),
        grid=(m // tm, nj),
        in_specs=[
            pl.BlockSpec((tm, d), lambda i, j: (i, 0)),
            pl.BlockSpec((halo, d), lambda i, j: (jnp.maximum(i * per - 1, 0), 0)),
            pl.BlockSpec((halo, d), lambda i, j: (jnp.minimum((i + 1) * per, n_halo - 1), 0)),
            pl.BlockSpec((None, 1, d), lambda i, j: (layer, 0, 0)),
            wspec(tn_qkv, 0), wspec(tn_conv, off_b), wspec(tn_conv, off_c), wspec(tn_conv, off_u),
            rowspec(tn_qkv), rowspec(tn_qkv),
            pl.BlockSpec((None, CONV_K, tn_conv), lambda i, j: (layer, 0, j)),
            rowspec(tn_conv),
        ],
        out_specs=(pl.BlockSpec((tm, tn_qkv), lambda i, j: (i, j)),
                   pl.BlockSpec((tm, tn_conv), lambda i, j: (i, j))),
        scratch_shapes=[pltpu.VMEM((tm + 2 * halo, d), BF16)],
        compiler_params=_params("parallel", "arbitrary"),
        name="in_projection",
    )(x, x, x, norm1, w_in, w_in, w_in, w_in, qkv_gain, qkv_normed, conv_w, conv_gain)


def _row_group_variants(n_rows):
    variants = []
    for kind in range(3):
        table = []
        for i in range(ROW_GROUP):
            row = []
            for j in range(KEY_ROWS):
                if kind == 0:
                    q_abs, k_abs = i, j
                elif kind == 1:
                    q_abs, k_abs = KEY_ROWS + i, KEY_ROWS - WIN_ROWS // 2 + j
                else:
                    q_abs, k_abs = n_rows - ROW_GROUP + i, n_rows - KEY_ROWS + j
                r0 = q_abs - WIN_ROWS // 2
                if kind != 1:
                    r0 = min(max(r0, 0), n_rows - WIN_ROWS)
                valid = r0 <= k_abs < r0 + WIN_ROWS
                row.append(k_abs - q_abs + WIN_ROWS - 1 if valid else None)
            table.append(row)
        variants.append(table)
    return variants


def _attn_kernel(q_ref, k_ref, v_ref, t_ref, g_ref, o_ref, bias_ref, s_ref, *, n_rows):
    n_groups = n_rows // ROW_GROUP
    gq = ROW_GROUP * GRID_W
    gk = KEY_ROWS * GRID_W

    left = lax.broadcasted_iota(jnp.int32, (GRID_W, 2 * GRID_W), 1) < GRID_W
    neg = jnp.full((GRID_W, 2 * GRID_W), NEG_INF, F32)
    for kind, table in enumerate(_row_group_variants(n_rows)):
        for i in range(ROW_GROUP):
            for jj in range(KEY_ROWS // 2):
                dl, dr = table[i][2 * jj], table[i][2 * jj + 1]
                if dl is None and dr is None:
                    blk = neg
                else:
                    lo = neg if dl is None else t_ref[dl]
                    hi = neg if dr is None else t_ref[dr]
                    blk = jnp.where(left, lo, hi)
                bias_ref[kind, i * GRID_W:(i + 1) * GRID_W,
                         jj * 2 * GRID_W:(jj + 1) * 2 * GRID_W] = blk

    gain = g_ref[...]

    def key_start(g):
        key_row0 = jnp.clip(g * ROW_GROUP - WIN_ROWS // 2, 0, n_rows - KEY_ROWS)
        return pl.multiple_of(key_row0 * GRID_W, GRID_W)

    def scores(g):
        kind = jnp.where(g == 0, 0, jnp.where(g == n_groups - 1, 2, 1))
        q = q_ref[pl.ds(pl.multiple_of(g * gq, gq), gq), :]
        k = k_ref[pl.ds(key_start(g), gk), :]
        s = lax.dot_general(q, k, (((1,), (1,)), ((), ())), preferred_element_type=F32)
        return s + bias_ref[kind]

    def finish(s, g):
        m = jnp.max(s, axis=-1, keepdims=True)
        p = jnp.exp2(s - m)
        l = jnp.sum(p, axis=-1, keepdims=True)
        v = v_ref[pl.ds(key_start(g), gk), :]
        o = jnp.dot(p.astype(BF16), v, preferred_element_type=F32) / l
        o_ref[pl.ds(pl.multiple_of(g * gq, gq), gq), :] = _rms(o, gain).astype(o_ref.dtype)

    s_ref[...] = scores(0)

    def body(t, carry):
        ga = 2 * t
        s_a = s_ref[...]
        s_b = scores(ga + 1)
        finish(s_a, ga)
        s_ref[...] = scores(jnp.minimum(ga + 2, n_groups - 1))
        finish(s_b, ga + 1)
        return carry

    lax.fori_loop(0, n_groups // 2, body, 0)


def _attention(qkv, bias_blocks, out_gain, layer, *, n_rows):
    b, seq, _ = qkv.shape
    assert seq == n_rows * GRID_W and n_rows % (2 * ROW_GROUP) == 0
    assert n_rows >= KEY_ROWS + ROW_GROUP and KEY_ROWS % 2 == 0
    blk = lambda off: pl.BlockSpec((None, seq, HEAD_DIM), lambda h, bi: (bi, 0, off + h))
    gq, gk = ROW_GROUP * GRID_W, KEY_ROWS * GRID_W
    return pl.pallas_call(
        functools.partial(_attn_kernel, n_rows=n_rows),
        out_shape=jax.ShapeDtypeStruct((b, seq, N_HEADS * HEAD_DIM), BF16),
        grid=(N_HEADS, b),
        in_specs=[
            blk(0), blk(N_HEADS), blk(2 * N_HEADS),
            pl.BlockSpec((None, N_REL_ROWS, GRID_W, 2 * GRID_W), lambda h, bi: (h, 0, 0, 0)),
            pl.BlockSpec((None, 1, HEAD_DIM), lambda h, bi: (layer, 0, h)),
        ],
        out_specs=pl.BlockSpec((None, seq, HEAD_DIM), lambda h, bi: (bi, 0, h)),
        scratch_shapes=[pltpu.VMEM((3, gq, gk), F32), pltpu.VMEM((gq, gk), F32)],
        compiler_params=_params("parallel", "parallel"),
        name="neighbourhood_attention",
    )(qkv, qkv, qkv, bias_blocks, out_gain)


def _bias_blocks(rpb_layer):
    c = np.arange(GRID_W)[:, None]
    kc = np.arange(GRID_W)[None, :]
    start = np.clip(c - WIN_COLS // 2, 0, GRID_W - WIN_COLS)
    valid = (kc >= start) & (kc < start + WIN_COLS)
    rel = np.clip(kc - c + WIN_COLS - 1, 0, N_REL_COLS - 1)
    t = jnp.where(jnp.asarray(valid), rpb_layer.astype(F32)[:, :, rel] * LOG2_E, NEG_INF)
    return jnp.concatenate([t, t], axis=-1)


def _outproj_kernel(x_ref, a_ref, c_ref, wa_ref, wc_ref, o_ref):
    acc = jnp.dot(a_ref[...], wa_ref[...], preferred_element_type=F32)
    acc = acc + jnp.dot(c_ref[...], wc_ref[...], preferred_element_type=F32)
    o_ref[...] = x_ref[...] + acc


def _out_projection(x, attn, conv, w_out, layer, *, tm):
    m, d = x.shape
    aw, cw = attn.shape[-1], conv.shape[-1]
    assert aw == cw and aw + cw == w_out.shape[1] and m % tm == 0
    return pl.pallas_call(
        _outproj_kernel,
        out_shape=jax.ShapeDtypeStruct((m, d), F32),
        grid=(m // tm,),
        in_specs=[
            pl.BlockSpec((tm, d), lambda i: (i, 0)),
            pl.BlockSpec((tm, aw), lambda i: (i, 0)),
            pl.BlockSpec((tm, cw), lambda i: (i, 0)),
            pl.BlockSpec((None, aw, d), lambda i: (layer, 0, 0)),
            pl.BlockSpec((None, cw, d), lambda i: (layer, 1, 0)),
        ],
        out_specs=pl.BlockSpec((tm, d), lambda i: (i, 0)),
        compiler_params=_params("parallel"),
        name="out_projection",
    )(x, attn, conv, w_out, w_out)


def _ffn_kernel(x_ref, g_ref, wg_ref, wu_ref, wd_ref, o_ref, hn_ref):
    j = pl.program_id(1)

    @pl.when(j == 0)
    def _():
        x = x_ref[...]
        hn_ref[...] = _rms(x, g_ref[...]).astype(BF16)
        o_ref[...] = x

    hn = hn_ref[...]
    gate = jnp.dot(hn, wg_ref[...], preferred_element_type=F32)
    up = jnp.dot(hn, wu_ref[...], preferred_element_type=F32)
    h = (gate * jax.nn.sigmoid(gate)) * up
    o_ref[...] += jnp.dot(h.astype(BF16), wd_ref[...], preferred_element_type=F32)


def _feed_forward(x, norm2, w_gate, w_up, w_down, layer, *, tm, tf):
    m, d = x.shape
    f = w_gate.shape[-1]
    assert m % tm == 0 and f % tf == 0
    return pl.pallas_call(
        _ffn_kernel,
        out_shape=jax.ShapeDtypeStruct((m, d), F32),
        grid=(m // tm, f // tf),
        in_specs=[
            pl.BlockSpec((tm, d), lambda i, j: (i, 0)),
            pl.BlockSpec((None, 1, d), lambda i, j: (layer, 0, 0)),
            pl.BlockSpec((None, d, tf), lambda i, j: (layer, 0, j)),
            pl.BlockSpec((None, d, tf), lambda i, j: (layer, 0, j)),
            pl.BlockSpec((None, tf, d), lambda i, j: (layer, j, 0)),
        ],
        out_specs=pl.BlockSpec((tm, d), lambda i, j: (i, 0)),
        scratch_shapes=[pltpu.VMEM((tm, d), BF16)],
        compiler_params=_params("parallel", "arbitrary"),
        name="feed_forward",
    )(x, norm2, w_gate, w_up, w_down)


TILES = dict(
    inproj=dict(tm=1024, tn_qkv=768, tn_conv=256),
    outproj=dict(tm=512),
    ffn=dict(tm=1024, tf=512),
)


def kernel(x_prompt, x_sample, norm1, w_in, q_gain, k_gain, rpb, conv_w, attn_out_gain,
           conv_out_gain, w_out, norm2, w_gate, w_up, w_down):
    depth, d, _ = w_in.shape
    attn_w = N_HEADS * HEAD_DIM
    w_in_b, w_out_b, w_gate_b, w_up_b, w_down_b = (
        w.astype(BF16) for w in (w_in, w_out, w_gate, w_up, w_down))
    row = lambda a: a.reshape(depth, 1, a.shape[-1])
    norm1_r, norm2_r = row(norm1), row(norm2)
    ag_r, cg_r = row(attn_out_gain), row(conv_out_gain)
    q_row = jnp.tile(q_gain.astype(F32) * (HEAD_DIM ** -0.5 * LOG2_E), (1, N_HEADS))
    k_row = jnp.tile(k_gain.astype(F32), (1, N_HEADS))
    ones = jnp.ones((depth, attn_w), F32)
    qkv_gain = row(jnp.concatenate([q_row, k_row, ones], axis=-1))
    qkv_normed = row(jnp.concatenate([ones, ones, 0.0 * ones], axis=-1))

    streams = []
    for x in (x_prompt, x_sample):
        b, seq, _ = x.shape
        streams.append([x.reshape(b * seq, d), b, seq])

    for layer in range(depth):
        bias_blocks = _bias_blocks(rpb[layer])
        for st in streams:
            x, b, seq = st
            qkv, conv = _in_projection(x, norm1_r, w_in_b, qkv_gain, qkv_normed, conv_w, cg_r,
                                       layer, seq=seq, **TILES["inproj"])
            attn = _attention(qkv.reshape(b, seq, -1), bias_blocks, ag_r, layer,
                              n_rows=seq // GRID_W)
            x = _out_projection(x, attn.reshape(b * seq, -1), conv, w_out_b, layer,
                                **TILES["outproj"])
            st[0] = _feed_forward(x, norm2_r, w_gate_b, w_up_b, w_down_b, layer, **TILES["ffn"])

    return tuple(x.reshape(b, seq, d) for x, b, seq in streams)
```

```python
import functools
import math

import numpy as np
import jax
import jax.numpy as jnp
from jax import lax
from jax.experimental import pallas as pl
from jax.experimental.pallas import tpu as pltpu

GRID_W = 64
N_HEADS = 8
HEAD_DIM = 128
CONV_GROUPS = 8
CONV_K = 3
WIN_ROWS = 8
WIN_COLS = 16
EPS = 1e-6
NEG_INF = -1e30
LOG2_E = math.log2(math.e)

BF16_ROWS = 16
NORM_CHUNK = 256
VMEM_LIMIT = 56 * 1024 * 1024

ROW_GROUP = 4
KEY_ROWS = ROW_GROUP + WIN_ROWS
GROUPS_PER_ITER = 8
N_REL_ROWS = 2 * WIN_ROWS - 1
N_REL_COLS = 2 * WIN_COLS - 1

F32 = jnp.float32
BF16 = jnp.bfloat16


def _params(*sem):
    return pltpu.CompilerParams(dimension_semantics=sem, vmem_limit_bytes=VMEM_LIMIT)


def _rms(x, gain):
    ms = jnp.mean(x * x, axis=-1, keepdims=True)
    return (x * lax.rsqrt(ms + EPS)) * gain


def _inproj_kernel(x_ref, xp_ref, xn_ref, g_ref, wa_ref, wb_ref, wc_ref, wu_ref, ag_ref, af_ref,
                   cw_ref, cg_ref, qkv_ref, conv_ref, hn_ref, *, tiles_per_seq):
    i = pl.program_id(0)
    j = pl.program_id(1)
    tm = x_ref.shape[0]
    halo = xp_ref.shape[0]

    def qkv_part(hn, r0, nr):
        acc = jnp.dot(hn, wa_ref[...], preferred_element_type=F32)
        gain = ag_ref[...]
        normed = af_ref[...] > 0.0
        for hh in range(acc.shape[-1] // HEAD_DIM):
            sl = slice(hh * HEAD_DIM, (hh + 1) * HEAD_DIM)
            a = acc[:, sl]
            qkv_ref[hh, r0:r0 + nr, :] = jnp.where(
                normed[:, sl], _rms(a, gain[:, sl]), a).astype(BF16)

    def conv_part():
        hn_all = hn_ref[...]
        hn = hn_ref[halo:halo + tm, :]
        cu = (jnp.dot(hn_all, wc_ref[...], preferred_element_type=F32)
              * jnp.dot(hn_all, wu_ref[...], preferred_element_type=F32))
        gate_b = jnp.dot(hn, wb_ref[...], preferred_element_type=F32)
        rows = cu.shape[0]
        before = pltpu.roll(cu, 1, 0)[halo:halo + tm]
        after = pltpu.roll(cu, rows - 1, 0)[halo:halo + tm]
        w = cw_ref[...]
        y = gate_b * (before * w[0:1, :] + cu[halo:halo + tm] * w[1:2, :] + after * w[2:3, :])
        cgain = cg_ref[...]
        for gi in range(y.shape[-1] // HEAD_DIM):
            sl = slice(gi * HEAD_DIM, (gi + 1) * HEAD_DIM)
            conv_ref[:, sl] = _rms(y[:, sl], cgain[:, sl]).astype(BF16)

    @pl.when(j == 0)
    def _():
        g = g_ref[...]
        first = i % tiles_per_seq == 0
        last = i % tiles_per_seq == tiles_per_seq - 1
        hn_ref[0:halo, :] = jnp.where(first, 0.0, _rms(xp_ref[...], g)).astype(BF16)
        hn_ref[halo + tm:, :] = jnp.where(last, 0.0, _rms(xn_ref[...], g)).astype(BF16)
        for r0 in range(0, tm, NORM_CHUNK):
            hn = _rms(x_ref[r0:r0 + NORM_CHUNK, :], g).astype(BF16)
            hn_ref[halo + r0:halo + r0 + NORM_CHUNK, :] = hn
            qkv_part(hn, r0, NORM_CHUNK)
        conv_part()

    @pl.when(j > 0)
    def _():
        qkv_part(hn_ref[halo:halo + tm, :], 0, tm)
        conv_part()


def _in_projection(x, norm1, w_in, qkv_gain, qkv_normed, conv_w, conv_gain, layer, *,
                   seq, tm, tn_qkv, tn_conv):
    m, d = x.shape
    attn_w = N_HEADS * HEAD_DIM
    conv_width = conv_w.shape[-1]
    assert w_in.shape[-1] == 3 * attn_w + 3 * conv_width
    assert conv_width // CONV_GROUPS == HEAD_DIM
    nj = 3 * attn_w // tn_qkv
    assert nj * tn_qkv == 3 * attn_w and nj * tn_conv == conv_width
    assert seq % tm == 0 and m % seq == 0 and tm % BF16_ROWS == 0
    halo = BF16_ROWS
    per = tm // halo
    n_halo = m // halo
    off_b = 3 * attn_w // tn_conv
    off_c = off_b + conv_width // tn_conv
    off_u = off_c + conv_width // tn_conv
    wspec = lambda width, off: pl.BlockSpec((None, d, width), lambda i, j: (layer, 0, off + j))
    rowspec = lambda width: pl.BlockSpec((None, 1, width), lambda i, j: (layer, 0, j))
    return pl.pallas_call(
        functools.partial(_inproj_kernel, tiles_per_seq=seq // tm),
        out_shape=(jax.ShapeDtypeStruct((3 * N_HEADS, m, HEAD_DIM), BF16),
                   jax.ShapeDtypeStruct((m, conv_width), BF16)),
        grid=(m // tm, nj),
        in_specs=[
            pl.BlockSpec((tm, d), lambda i, j: (i, 0)),
            pl.BlockSpec((halo, d), lambda i, j: (jnp.maximum(i * per - 1, 0), 0)),
            pl.BlockSpec((halo, d), lambda i, j: (jnp.minimum((i + 1) * per, n_halo - 1), 0)),
            pl.BlockSpec((None, 1, d), lambda i, j: (layer, 0, 0)),
            wspec(tn_qkv, 0), wspec(tn_conv, off_b), wspec(tn_conv, off_c), wspec(tn_conv, off_u),
            rowspec(tn_qkv), rowspec(tn_qkv),
            pl.BlockSpec((None, CONV_K, tn_conv), lambda i, j: (layer, 0, j)),
            rowspec(tn_conv),
        ],
        out_specs=(pl.BlockSpec((tn_qkv // HEAD_DIM, tm, HEAD_DIM), lambda i, j: (j, i, 0)),
                   pl.BlockSpec((tm, tn_conv), lambda i, j: (i, j))),
        scratch_shapes=[pltpu.VMEM((tm + 2 * halo, d), BF16)],
        compiler_params=_params("parallel", "arbitrary"),
        name="in_projection",
    )(x, x, x, norm1, w_in, w_in, w_in, w_in, qkv_gain, qkv_normed, conv_w, conv_gain)


def _row_group_variants(n_rows):
    variants = []
    for kind in range(3):
        table = []
        for i in range(ROW_GROUP):
            row = []
            for j in range(KEY_ROWS):
                if kind == 0:
                    q_abs, k_abs = i, j
                elif kind == 1:
                    q_abs, k_abs = KEY_ROWS + i, KEY_ROWS - WIN_ROWS // 2 + j
                else:
                    q_abs, k_abs = n_rows - ROW_GROUP + i, n_rows - KEY_ROWS + j
                r0 = q_abs - WIN_ROWS // 2
                if kind != 1:
                    r0 = min(max(r0, 0), n_rows - WIN_ROWS)
                valid = r0 <= k_abs < r0 + WIN_ROWS
                row.append(k_abs - q_abs + WIN_ROWS - 1 if valid else None)
            table.append(row)
        variants.append(table)
    return variants


def _attn_kernel(q_ref, k_ref, v_ref, t_ref, g_ref, o_ref, bias_ref, s_ref, acc_ref, *, n_rows):
    n_groups = n_rows // ROW_GROUP
    gq = ROW_GROUP * GRID_W
    gk = KEY_ROWS * GRID_W

    left = lax.broadcasted_iota(jnp.int32, (GRID_W, 2 * GRID_W), 1) < GRID_W
    neg = jnp.full((GRID_W, 2 * GRID_W), NEG_INF, F32)
    for kind, table in enumerate(_row_group_variants(n_rows)):
        for i in range(ROW_GROUP):
            for jj in range(KEY_ROWS // 2):
                dl, dr = table[i][2 * jj], table[i][2 * jj + 1]
                if dl is None and dr is None:
                    blk = neg
                else:
                    lo = neg if dl is None else t_ref[dl]
                    hi = neg if dr is None else t_ref[dr]
                    blk = jnp.where(left, lo, hi)
                bias_ref[kind, i * GRID_W:(i + 1) * GRID_W,
                         jj * 2 * GRID_W:(jj + 1) * 2 * GRID_W] = blk

    gain = g_ref[...]

    def key_start(g):
        key_row0 = jnp.clip(g * ROW_GROUP - WIN_ROWS // 2, 0, n_rows - KEY_ROWS)
        return pl.multiple_of(key_row0 * GRID_W, GRID_W)

    def scores(g):
        kind = jnp.where(g == 0, 0, jnp.where(g == n_groups - 1, 2, 1))
        q = q_ref[pl.ds(pl.multiple_of(g * gq, gq), gq), :]
        k = k_ref[pl.ds(key_start(g), gk), :]
        s = lax.dot_general(q, k, (((1,), (1,)), ((), ())), preferred_element_type=F32)
        return s + bias_ref[kind]

    def attend(s, g):
        m = jnp.max(s, axis=-1, keepdims=True)
        p = jnp.exp2(s - m)
        l = jnp.sum(p, axis=-1, keepdims=True)
        v = v_ref[pl.ds(key_start(g), gk), :]
        return jnp.dot(p.astype(BF16), v, preferred_element_type=F32) / l

    def emit(o, g):
        o_ref[pl.ds(pl.multiple_of(g * gq, gq), gq), :] = _rms(o, gain).astype(o_ref.dtype)

    s_ref[...] = scores(0)
    acc_ref[...] = jnp.zeros_like(acc_ref)

    def body(t, carry):
        g0 = GROUPS_PER_ITER * t
        emit(acc_ref[...], jnp.maximum(g0 - 1, 0))
        s_cur = s_ref[...]
        for u in range(GROUPS_PER_ITER):
            g = g0 + u
            s_next = scores(jnp.minimum(g + 1, n_groups - 1))
            o = attend(s_cur, g)
            if u < GROUPS_PER_ITER - 1:
                emit(o, g)
            else:
                acc_ref[...] = o
            s_cur = s_next
        s_ref[...] = s_cur
        return carry

    lax.fori_loop(0, n_groups // GROUPS_PER_ITER, body, 0)
    emit(acc_ref[...], n_groups - 1)


def _attention(qkv, bias_blocks, out_gain, layer, *, n_rows):
    _, m, _ = qkv.shape
    seq = n_rows * GRID_W
    b = m // seq
    assert b * seq == m and n_rows % (GROUPS_PER_ITER * ROW_GROUP) == 0
    assert n_rows >= KEY_ROWS + ROW_GROUP and KEY_ROWS % 2 == 0
    blk = lambda off: pl.BlockSpec((None, seq, HEAD_DIM), lambda h, bi: (off + h, bi, 0))
    gq, gk = ROW_GROUP * GRID_W, KEY_ROWS * GRID_W
    return pl.pallas_call(
        functools.partial(_attn_kernel, n_rows=n_rows),
        out_shape=jax.ShapeDtypeStruct((N_HEADS, m, HEAD_DIM), BF16),
        grid=(N_HEADS, b),
        in_specs=[
            blk(0), blk(N_HEADS), blk(2 * N_HEADS),
            pl.BlockSpec((None, N_REL_ROWS, GRID_W, 2 * GRID_W), lambda h, bi: (h, 0, 0, 0)),
            pl.BlockSpec((None, 1, HEAD_DIM), lambda h, bi: (layer, 0, h)),
        ],
        out_specs=pl.BlockSpec((None, seq, HEAD_DIM), lambda h, bi: (h, bi, 0)),
        scratch_shapes=[pltpu.VMEM((3, gq, gk), F32), pltpu.VMEM((gq, gk), F32),
                        pltpu.VMEM((gq, HEAD_DIM), F32)],
        compiler_params=_params("parallel", "parallel"),
        name="neighbourhood_attention",
    )(qkv, qkv, qkv, bias_blocks, out_gain)


def _bias_blocks(rpb_layer):
    c = np.arange(GRID_W)[:, None]
    kc = np.arange(GRID_W)[None, :]
    start = np.clip(c - WIN_COLS // 2, 0, GRID_W - WIN_COLS)
    valid = (kc >= start) & (kc < start + WIN_COLS)
    rel = np.clip(kc - c + WIN_COLS - 1, 0, N_REL_COLS - 1)
    t = jnp.where(jnp.asarray(valid), rpb_layer.astype(F32)[:, :, rel] * LOG2_E, NEG_INF)
    return jnp.concatenate([t, t], axis=-1)


def _outproj_kernel(x_ref, a_ref, c_ref, wa_ref, wc_ref, o_ref):
    attn = jnp.concatenate([a_ref[h] for h in range(a_ref.shape[0])], axis=-1)
    acc = jnp.dot(attn, wa_ref[...], preferred_element_type=F32)
    acc = acc + jnp.dot(c_ref[...], wc_ref[...], preferred_element_type=F32)
    o_ref[...] = x_ref[...] + acc


def _out_projection(x, attn, conv, w_out, layer, *, tm):
    m, d = x.shape
    n_heads, _, head_dim = attn.shape
    aw, cw = n_heads * head_dim, conv.shape[-1]
    assert aw == cw and aw + cw == w_out.shape[1] and m % tm == 0
    return pl.pallas_call(
        _outproj_kernel,
        out_shape=jax.ShapeDtypeStruct((m, d), F32),
        grid=(m // tm,),
        in_specs=[
            pl.BlockSpec((tm, d), lambda i: (i, 0)),
            pl.BlockSpec((n_heads, tm, head_dim), lambda i: (0, i, 0)),
            pl.BlockSpec((tm, cw), lambda i: (i, 0)),
            pl.BlockSpec((None, aw, d), lambda i: (layer, 0, 0)),
            pl.BlockSpec((None, cw, d), lambda i: (layer, 1, 0)),
        ],
        out_specs=pl.BlockSpec((tm, d), lambda i: (i, 0)),
        compiler_params=_params("parallel"),
        name="out_projection",
    )(x, attn, conv, w_out, w_out)


def _ffn_kernel(x_ref, g_ref, wg_ref, wu_ref, wd_ref, o_ref, hn_ref):
    j = pl.program_id(1)
    tm = x_ref.shape[0]

    def swiglu(hn):
        gate = jnp.dot(hn, wg_ref[...], preferred_element_type=F32)
        up = jnp.dot(hn, wu_ref[...], preferred_element_type=F32)
        h = (gate * jax.nn.sigmoid(gate)) * up
        return jnp.dot(h.astype(BF16), wd_ref[...], preferred_element_type=F32)

    @pl.when(j == 0)
    def _():
        g = g_ref[...]
        for r0 in range(0, tm, NORM_CHUNK):
            rows = slice(r0, r0 + NORM_CHUNK)
            x = x_ref[rows, :]
            hn = _rms(x, g).astype(BF16)
            hn_ref[rows, :] = hn
            o_ref[rows, :] = x + swiglu(hn)

    @pl.when(j > 0)
    def _():
        o_ref[...] += swiglu(hn_ref[...])


def _feed_forward(x, norm2, w_gate, w_up, w_down, layer, *, tm, tf):
    m, d = x.shape
    f = w_gate.shape[-1]
    assert m % tm == 0 and f % tf == 0
    return pl.pallas_call(
        _ffn_kernel,
        out_shape=jax.ShapeDtypeStruct((m, d), F32),
        grid=(m // tm, f // tf),
        in_specs=[
            pl.BlockSpec((tm, d), lambda i, j: (i, 0)),
            pl.BlockSpec((None, 1, d), lambda i, j: (layer, 0, 0)),
            pl.BlockSpec((None, d, tf), lambda i, j: (layer, 0, j)),
            pl.BlockSpec((None, d, tf), lambda i, j: (layer, 0, j)),
            pl.BlockSpec((None, tf, d), lambda i, j: (layer, j, 0)),
        ],
        out_specs=pl.BlockSpec((tm, d), lambda i, j: (i, 0)),
        scratch_shapes=[pltpu.VMEM((tm, d), BF16)],
        compiler_params=_params("parallel", "arbitrary"),
        name="feed_forward",
    )(x, norm2, w_gate, w_up, w_down)


TILES = dict(
    inproj=dict(tm=1024, tn_qkv=768, tn_conv=256),
    outproj=dict(tm=512),
    ffn=dict(tm=1024, tf=512),
)


def kernel(x_prompt, x_sample, norm1, w_in, q_gain, k_gain, rpb, conv_w, attn_out_gain,
           conv_out_gain, w_out, norm2, w_gate, w_up, w_down):
    depth, d, _ = w_in.shape
    attn_w = N_HEADS * HEAD_DIM
    w_in_b, w_out_b, w_gate_b, w_up_b, w_down_b = (
        w.astype(BF16) for w in (w_in, w_out, w_gate, w_up, w_down))
    row = lambda a: a.reshape(depth, 1, a.shape[-1])
    norm1_r, norm2_r = row(norm1), row(norm2)
    ag_r, cg_r = row(attn_out_gain), row(conv_out_gain)
    q_row = jnp.tile(q_gain.astype(F32) * (HEAD_DIM ** -0.5 * LOG2_E), (1, N_HEADS))
    k_row = jnp.tile(k_gain.astype(F32), (1, N_HEADS))
    ones = jnp.ones((depth, attn_w), F32)
    qkv_gain = row(jnp.concatenate([q_row, k_row, ones], axis=-1))
    qkv_normed = row(jnp.concatenate([ones, ones, 0.0 * ones], axis=-1))

    streams = []
    for x in (x_prompt, x_sample):
        b, seq, _ = x.shape
        streams.append([x.reshape(b * seq, d), b, seq])

    for layer in range(depth):
        bias_blocks = _bias_blocks(rpb[layer])
        for st in streams:
            x, b, seq = st
            qkv, conv = _in_projection(x, norm1_r, w_in_b, qkv_gain, qkv_normed, conv_w, cg_r,
                                       layer, seq=seq, **TILES["inproj"])
            attn = _attention(qkv, bias_blocks, ag_r, layer, n_rows=seq // GRID_W)
            x = _out_projection(x, attn, conv, w_out_b, layer, **TILES["outproj"])
            st[0] = _feed_forward(x, norm2_r, w_gate_b, w_up_b, w_down_b, layer, **TILES["ffn"])

    return tuple(x.reshape(b, seq, d) for x, b, seq in streams)
```

```python
import functools
import math

import numpy as np
import jax
import jax.numpy as jnp
from jax import lax
from jax.experimental import pallas as pl
from jax.experimental.pallas import tpu as pltpu

GRID_W = 64
N_HEADS = 8
HEAD_DIM = 128
CONV_GROUPS = 8
CONV_K = 3
WIN_ROWS = 8
WIN_COLS = 16
EPS = 1e-6
NEG_INF = -1e30
LOG2_E = math.log2(math.e)

BF16_ROWS = 16
NORM_CHUNK = 256
VMEM_LIMIT = 56 * 1024 * 1024

ROW_GROUP = 4
KEY_ROWS = ROW_GROUP + WIN_ROWS
GROUPS_PER_ITER = 8
N_REL_ROWS = 2 * WIN_ROWS - 1
N_REL_COLS = 2 * WIN_COLS - 1

F32 = jnp.float32
BF16 = jnp.bfloat16


def _params(*sem):
    return pltpu.CompilerParams(dimension_semantics=sem, vmem_limit_bytes=VMEM_LIMIT)


def _rms(x, gain):
    ms = jnp.mean(x * x, axis=-1, keepdims=True)
    return (x * lax.rsqrt(ms + EPS)) * gain


def _inproj_kernel(x_ref, xp_ref, xn_ref, g_ref, wq_ref, wk_ref, wv_ref, wb_ref, wc_ref, wu_ref,
                   qg_ref, kg_ref, cw_ref, cg_ref, q_ref, k_ref, v_ref, conv_ref, hn_ref,
                   *, tiles_per_seq):
    i = pl.program_id(0)
    j = pl.program_id(1)
    tm = x_ref.shape[0]
    halo = xp_ref.shape[0]

    def head_norm(w_ref, gain_ref, o_ref, hn, r0, nr):
        acc = jnp.dot(hn, w_ref[...], preferred_element_type=F32)
        gain = gain_ref[...]
        for hh in range(acc.shape[-1] // HEAD_DIM):
            a = acc[:, hh * HEAD_DIM:(hh + 1) * HEAD_DIM]
            o_ref[hh, r0:r0 + nr, :] = _rms(a, gain).astype(BF16)

    def v_part(hn, r0, nr):
        acc = jnp.dot(hn, wv_ref[...], preferred_element_type=F32)
        for hh in range(acc.shape[-1] // HEAD_DIM):
            v_ref[hh, r0:r0 + nr, :] = acc[:, hh * HEAD_DIM:(hh + 1) * HEAD_DIM].astype(BF16)

    def conv_part():
        hn_all = hn_ref[...]
        hn = hn_ref[halo:halo + tm, :]
        cu = (jnp.dot(hn_all, wc_ref[...], preferred_element_type=F32)
              * jnp.dot(hn_all, wu_ref[...], preferred_element_type=F32))
        gate_b = jnp.dot(hn, wb_ref[...], preferred_element_type=F32)
        rows = cu.shape[0]
        before = pltpu.roll(cu, 1, 0)[halo:halo + tm]
        after = pltpu.roll(cu, rows - 1, 0)[halo:halo + tm]
        w = cw_ref[...]
        y = gate_b * (before * w[0:1, :] + cu[halo:halo + tm] * w[1:2, :] + after * w[2:3, :])
        cgain = cg_ref[...]
        for gi in range(y.shape[-1] // HEAD_DIM):
            sl = slice(gi * HEAD_DIM, (gi + 1) * HEAD_DIM)
            conv_ref[:, sl] = _rms(y[:, sl], cgain[:, sl]).astype(BF16)

    @pl.when(j == 0)
    def _():
        g = g_ref[...]
        first = i % tiles_per_seq == 0
        last = i % tiles_per_seq == tiles_per_seq - 1
        hn_ref[0:halo, :] = jnp.where(first, 0.0, _rms(xp_ref[...], g)).astype(BF16)
        hn_ref[halo + tm:, :] = jnp.where(last, 0.0, _rms(xn_ref[...], g)).astype(BF16)
        for r0 in range(0, tm, NORM_CHUNK):
            hn = _rms(x_ref[r0:r0 + NORM_CHUNK, :], g).astype(BF16)
            hn_ref[halo + r0:halo + r0 + NORM_CHUNK, :] = hn
            head_norm(wq_ref, qg_ref, q_ref, hn, r0, NORM_CHUNK)
            head_norm(wk_ref, kg_ref, k_ref, hn, r0, NORM_CHUNK)
        conv_part()
        v_part(hn_ref[halo:halo + tm, :], 0, tm)

    @pl.when(j > 0)
    def _():
        conv_part()
        hn = hn_ref[halo:halo + tm, :]
        head_norm(wq_ref, qg_ref, q_ref, hn, 0, tm)
        head_norm(wk_ref, kg_ref, k_ref, hn, 0, tm)
        v_part(hn, 0, tm)


def _in_projection(x, norm1, w_in, q_gain, k_gain, conv_w, conv_gain, layer, *, seq, tm, tn):
    m, d = x.shape
    attn_w = N_HEADS * HEAD_DIM
    conv_width = conv_w.shape[-1]
    assert w_in.shape[-1] == 3 * attn_w + 3 * conv_width and attn_w == conv_width
    assert conv_width // CONV_GROUPS == HEAD_DIM and attn_w % tn == 0 and tn % HEAD_DIM == 0
    assert seq % tm == 0 and m % seq == 0 and tm % NORM_CHUNK == 0
    nj = attn_w // tn
    halo = BF16_ROWS
    per = tm // halo
    n_halo = m // halo
    wspec = lambda part: pl.BlockSpec((None, d, tn), lambda i, j: (layer, 0, part * nj + j))
    gainspec = pl.BlockSpec((None, 1, HEAD_DIM), lambda i, j: (layer, 0, 0))
    headspec = pl.BlockSpec((tn // HEAD_DIM, tm, HEAD_DIM), lambda i, j: (j, i, 0))
    heads = jax.ShapeDtypeStruct((N_HEADS, m, HEAD_DIM), BF16)
    return pl.pallas_call(
        functools.partial(_inproj_kernel, tiles_per_seq=seq // tm),
        out_shape=(heads, heads, heads, jax.ShapeDtypeStruct((m, conv_width), BF16)),
        grid=(m // tm, nj),
        in_specs=[
            pl.BlockSpec((tm, d), lambda i, j: (i, 0)),
            pl.BlockSpec((halo, d), lambda i, j: (jnp.maximum(i * per - 1, 0), 0)),
            pl.BlockSpec((halo, d), lambda i, j: (jnp.minimum((i + 1) * per, n_halo - 1), 0)),
            pl.BlockSpec((None, 1, d), lambda i, j: (layer, 0, 0)),
            wspec(0), wspec(1), wspec(2), wspec(3), wspec(4), wspec(5),
            gainspec, gainspec,
            pl.BlockSpec((None, CONV_K, tn), lambda i, j: (layer, 0, j)),
            pl.BlockSpec((None, 1, tn), lambda i, j: (layer, 0, j)),
        ],
        out_specs=(headspec, headspec, headspec, pl.BlockSpec((tm, tn), lambda i, j: (i, j))),
        scratch_shapes=[pltpu.VMEM((tm + 2 * halo, d), BF16)],
        compiler_params=_params("parallel", "arbitrary"),
        name="in_projection",
    )(x, x, x, norm1, w_in, w_in, w_in, w_in, w_in, w_in, q_gain, k_gain, conv_w, conv_gain)


def _row_group_variants(n_rows):
    variants = []
    for kind in range(3):
        table = []
        for i in range(ROW_GROUP):
            row = []
            for j in range(KEY_ROWS):
                if kind == 0:
                    q_abs, k_abs = i, j
                elif kind == 1:
                    q_abs, k_abs = KEY_ROWS + i, KEY_ROWS - WIN_ROWS // 2 + j
                else:
                    q_abs, k_abs = n_rows - ROW_GROUP + i, n_rows - KEY_ROWS + j
                r0 = q_abs - WIN_ROWS // 2
                if kind != 1:
                    r0 = min(max(r0, 0), n_rows - WIN_ROWS)
                valid = r0 <= k_abs < r0 + WIN_ROWS
                row.append(k_abs - q_abs + WIN_ROWS - 1 if valid else None)
            table.append(row)
        variants.append(table)
    return variants


def _attn_kernel(q_ref, k_ref, v_ref, t_ref, g_ref, o_ref, bias_ref, s_ref, acc_ref, *, n_rows):
    n_groups = n_rows // ROW_GROUP
    gq = ROW_GROUP * GRID_W
    gk = KEY_ROWS * GRID_W

    left = lax.broadcasted_iota(jnp.int32, (GRID_W, 2 * GRID_W), 1) < GRID_W
    neg = jnp.full((GRID_W, 2 * GRID_W), NEG_INF, F32)
    for kind, table in enumerate(_row_group_variants(n_rows)):
        for i in range(ROW_GROUP):
            for jj in range(KEY_ROWS // 2):
                dl, dr = table[i][2 * jj], table[i][2 * jj + 1]
                if dl is None and dr is None:
                    blk = neg
                else:
                    lo = neg if dl is None else t_ref[dl]
                    hi = neg if dr is None else t_ref[dr]
                    blk = jnp.where(left, lo, hi)
                bias_ref[kind, i * GRID_W:(i + 1) * GRID_W,
                         jj * 2 * GRID_W:(jj + 1) * 2 * GRID_W] = blk

    gain = g_ref[...]

    def key_start(g):
        key_row0 = jnp.clip(g * ROW_GROUP - WIN_ROWS // 2, 0, n_rows - KEY_ROWS)
        return pl.multiple_of(key_row0 * GRID_W, GRID_W)

    def scores(g):
        kind = jnp.where(g == 0, 0, jnp.where(g == n_groups - 1, 2, 1))
        q = q_ref[pl.ds(pl.multiple_of(g * gq, gq), gq), :]
        k = k_ref[pl.ds(key_start(g), gk), :]
        s = lax.dot_general(q, k, (((1,), (1,)), ((), ())), preferred_element_type=F32)
        return s + bias_ref[kind]

    def attend(s, g):
        m = jnp.max(s, axis=-1, keepdims=True)
        p = jnp.exp2(s - m)
        l = jnp.sum(p, axis=-1, keepdims=True)
        v = v_ref[pl.ds(key_start(g), gk), :]
        return jnp.dot(p.astype(BF16), v, preferred_element_type=F32) / l

    def emit(o, g):
        o_ref[pl.ds(pl.multiple_of(g * gq, gq), gq), :] = _rms(o, gain).astype(o_ref.dtype)

    s_ref[...] = scores(0)
    acc_ref[...] = jnp.zeros_like(acc_ref)

    def body(t, carry):
        g0 = GROUPS_PER_ITER * t
        emit(acc_ref[...], jnp.maximum(g0 - 1, 0))
        s_cur = s_ref[...]
        for u in range(GROUPS_PER_ITER):
            g = g0 + u
            s_next = scores(jnp.minimum(g + 1, n_groups - 1))
            o = attend(s_cur, g)
            if u < GROUPS_PER_ITER - 1:
                emit(o, g)
            else:
                acc_ref[...] = o
            s_cur = s_next
        s_ref[...] = s_cur
        return carry

    lax.fori_loop(0, n_groups // GROUPS_PER_ITER, body, 0)
    emit(acc_ref[...], n_groups - 1)


def _attention(q, k, v, bias_blocks, out_gain, layer, *, n_rows):
    _, m, _ = q.shape
    seq = n_rows * GRID_W
    b = m // seq
    assert b * seq == m and n_rows % (GROUPS_PER_ITER * ROW_GROUP) == 0
    assert n_rows >= KEY_ROWS + ROW_GROUP and KEY_ROWS % 2 == 0
    blk = pl.BlockSpec((None, seq, HEAD_DIM), lambda h, bi: (h, bi, 0))
    gq, gk = ROW_GROUP * GRID_W, KEY_ROWS * GRID_W
    return pl.pallas_call(
        functools.partial(_attn_kernel, n_rows=n_rows),
        out_shape=jax.ShapeDtypeStruct((N_HEADS, m, HEAD_DIM), BF16),
        grid=(N_HEADS, b),
        in_specs=[
            blk, blk, blk,
            pl.BlockSpec((None, N_REL_ROWS, GRID_W, 2 * GRID_W), lambda h, bi: (h, 0, 0, 0)),
            pl.BlockSpec((None, 1, HEAD_DIM), lambda h, bi: (layer, 0, h)),
        ],
        out_specs=blk,
        scratch_shapes=[pltpu.VMEM((3, gq, gk), F32), pltpu.VMEM((gq, gk), F32),
                        pltpu.VMEM((gq, HEAD_DIM), F32)],
        compiler_params=_params("parallel", "parallel"),
        name="neighbourhood_attention",
    )(q, k, v, bias_blocks, out_gain)


def _bias_blocks(rpb_layer):
    c = np.arange(GRID_W)[:, None]
    kc = np.arange(GRID_W)[None, :]
    start = np.clip(c - WIN_COLS // 2, 0, GRID_W - WIN_COLS)
    valid = (kc >= start) & (kc < start + WIN_COLS)
    rel = np.clip(kc - c + WIN_COLS - 1, 0, N_REL_COLS - 1)
    t = jnp.where(jnp.asarray(valid), rpb_layer.astype(F32)[:, :, rel] * LOG2_E, NEG_INF)
    return jnp.concatenate([t, t], axis=-1)


def _outproj_kernel(x_ref, a_ref, c_ref, wa_ref, wc_ref, o_ref):
    attn = jnp.concatenate([a_ref[h] for h in range(a_ref.shape[0])], axis=-1)
    acc = jnp.dot(attn, wa_ref[...], preferred_element_type=F32)
    acc = acc + jnp.dot(c_ref[...], wc_ref[...], preferred_element_type=F32)
    o_ref[...] = x_ref[...] + acc


def _out_projection(x, attn, conv, w_out, layer, *, tm):
    m, d = x.shape
    n_heads, _, head_dim = attn.shape
    aw, cw = n_heads * head_dim, conv.shape[-1]
    assert aw == cw and aw + cw == w_out.shape[1] and m % tm == 0
    return pl.pallas_call(
        _outproj_kernel,
        out_shape=jax.ShapeDtypeStruct((m, d), F32),
        grid=(m // tm,),
        in_specs=[
            pl.BlockSpec((tm, d), lambda i: (i, 0)),
            pl.BlockSpec((n_heads, tm, head_dim), lambda i: (0, i, 0)),
            pl.BlockSpec((tm, cw), lambda i: (i, 0)),
            pl.BlockSpec((None, aw, d), lambda i: (layer, 0, 0)),
            pl.BlockSpec((None, cw, d), lambda i: (layer, 1, 0)),
        ],
        out_specs=pl.BlockSpec((tm, d), lambda i: (i, 0)),
        compiler_params=_params("parallel"),
        name="out_projection",
    )(x, attn, conv, w_out, w_out)


def _ffn_kernel(x_ref, g_ref, wg_ref, wu_ref, wd_ref, o_ref, hn_ref):
    j = pl.program_id(1)
    tm = x_ref.shape[0]

    def swiglu(hn):
        gate = jnp.dot(hn, wg_ref[...], preferred_element_type=F32)
        up = jnp.dot(hn, wu_ref[...], preferred_element_type=F32)
        h = (gate * jax.nn.sigmoid(gate)) * up
        return jnp.dot(h.astype(BF16), wd_ref[...], preferred_element_type=F32)

    @pl.when(j == 0)
    def _():
        g = g_ref[...]
        for r0 in range(0, tm, NORM_CHUNK):
            rows = slice(r0, r0 + NORM_CHUNK)
            x = x_ref[rows, :]
            hn = _rms(x, g).astype(BF16)
            hn_ref[rows, :] = hn
            o_ref[rows, :] = x + swiglu(hn)

    @pl.when(j > 0)
    def _():
        o_ref[...] += swiglu(hn_ref[...])


def _feed_forward(x, norm2, w_gate, w_up, w_down, layer, *, tm, tf):
    m, d = x.shape
    f = w_gate.shape[-1]
    assert m % tm == 0 and f % tf == 0
    return pl.pallas_call(
        _ffn_kernel,
        out_shape=jax.ShapeDtypeStruct((m, d), F32),
        grid=(m // tm, f // tf),
        in_specs=[
            pl.BlockSpec((tm, d), lambda i, j: (i, 0)),
            pl.BlockSpec((None, 1, d), lambda i, j: (layer, 0, 0)),
            pl.BlockSpec((None, d, tf), lambda i, j: (layer, 0, j)),
            pl.BlockSpec((None, d, tf), lambda i, j: (layer, 0, j)),
            pl.BlockSpec((None, tf, d), lambda i, j: (layer, j, 0)),
        ],
        out_specs=pl.BlockSpec((tm, d), lambda i, j: (i, 0)),
        scratch_shapes=[pltpu.VMEM((tm, d), BF16)],
        compiler_params=_params("parallel", "arbitrary"),
        name="feed_forward",
    )(x, norm2, w_gate, w_up, w_down)


TILES = dict(
    inproj=dict(tm=1024, tn=256),
    outproj=dict(tm=512),
    ffn=dict(tm=1024, tf=512),
)


def kernel(x_prompt, x_sample, norm1, w_in, q_gain, k_gain, rpb, conv_w, attn_out_gain,
           conv_out_gain, w_out, norm2, w_gate, w_up, w_down):
    depth, d, _ = w_in.shape
    w_in_b, w_out_b, w_gate_b, w_up_b, w_down_b = (
        w.astype(BF16) for w in (w_in, w_out, w_gate, w_up, w_down))
    row = lambda a: a.reshape(depth, 1, a.shape[-1])
    norm1_r, norm2_r = row(norm1), row(norm2)
    ag_r, cg_r = row(attn_out_gain), row(conv_out_gain)
    qg_r = row(q_gain.astype(F32) * (HEAD_DIM ** -0.5 * LOG2_E))
    kg_r = row(k_gain.astype(F32))

    streams = []
    for x in (x_prompt, x_sample):
        b, seq, _ = x.shape
        streams.append([x.reshape(b * seq, d), b, seq])

    for layer in range(depth):
        bias_blocks = _bias_blocks(rpb[layer])
        for st in streams:
            x, b, seq = st
            q, k, v, conv = _in_projection(x, norm1_r, w_in_b, qg_r, kg_r, conv_w, cg_r, layer,
                                           seq=seq, **TILES["inproj"])
            attn = _attention(q, k, v, bias_blocks, ag_r, layer, n_rows=seq // GRID_W)
            x = _out_projection(x, attn, conv, w_out_b, layer, **TILES["outproj"])
            st[0] = _feed_forward(x, norm2_r, w_gate_b, w_up_b, w_down_b, layer, **TILES["ffn"])

    return tuple(x.reshape(b, seq, d) for x, b, seq in streams)
```

```python
import functools
import math

import numpy as np
import jax
import jax.numpy as jnp
from jax import lax
from jax.experimental import pallas as pl
from jax.experimental.pallas import tpu as pltpu

GRID_W = 64
N_HEADS = 8
HEAD_DIM = 128
CONV_GROUPS = 8
CONV_K = 3
WIN_ROWS = 8
WIN_COLS = 16
EPS = 1e-6
NEG_INF = -1e30
LOG2_E = math.log2(math.e)

BF16_ROWS = 16
NORM_CHUNK = 256
VT_CHUNK = 256
VT_PAD = BF16_ROWS
VMEM_LIMIT = 56 * 1024 * 1024

ROW_GROUP = 4
KEY_ROWS = ROW_GROUP + WIN_ROWS
GROUPS_PER_ITER = 8
N_REL_ROWS = 2 * WIN_ROWS - 1
N_REL_COLS = 2 * WIN_COLS - 1

F32 = jnp.float32
BF16 = jnp.bfloat16


def _params(*sem):
    return pltpu.CompilerParams(dimension_semantics=sem, vmem_limit_bytes=VMEM_LIMIT)


def _rms(x, gain):
    ms = jnp.mean(x * x, axis=-1, keepdims=True)
    return (x * lax.rsqrt(ms + EPS)) * gain


def _inproj_kernel(x_ref, xp_ref, xn_ref, g_ref, wq_ref, wk_ref, wv_ref, wb_ref, wc_ref, wu_ref,
                   qg_ref, kg_ref, cw_ref, cg_ref, q_ref, k_ref, vt_ref, conv_ref, hn_ref,
                   *, tiles_per_seq):
    i = pl.program_id(0)
    j = pl.program_id(1)
    tm = x_ref.shape[0]
    halo = xp_ref.shape[0]

    def head_norm(w_ref, gain_ref, o_ref, hn, r0, nr):
        acc = jnp.dot(hn, w_ref[...], preferred_element_type=F32)
        gain = gain_ref[...]
        for hh in range(acc.shape[-1] // HEAD_DIM):
            a = acc[:, hh * HEAD_DIM:(hh + 1) * HEAD_DIM]
            o_ref[hh, r0:r0 + nr, :] = _rms(a, gain).astype(BF16)

    def v_part(hn):
        acc = lax.dot_general(wv_ref[...], hn, (((1,), (1,)), ((), ())),
                              preferred_element_type=F32)
        ones = jnp.ones((VT_PAD, VT_CHUNK), BF16)
        for hh in range(acc.shape[0] // HEAD_DIM):
            for c in range(acc.shape[1] // VT_CHUNK):
                blk = acc[hh * HEAD_DIM:(hh + 1) * HEAD_DIM, c * VT_CHUNK:(c + 1) * VT_CHUNK]
                vt_ref[hh, c, 0:HEAD_DIM, :] = blk.astype(BF16)
                vt_ref[hh, c, HEAD_DIM:, :] = ones

    def conv_part():
        hn_all = hn_ref[...]
        hn = hn_ref[halo:halo + tm, :]
        cu = (jnp.dot(hn_all, wc_ref[...], preferred_element_type=F32)
              * jnp.dot(hn_all, wu_ref[...], preferred_element_type=F32))
        gate_b = jnp.dot(hn, wb_ref[...], preferred_element_type=F32)
        rows = cu.shape[0]
        before = pltpu.roll(cu, 1, 0)[halo:halo + tm]
        after = pltpu.roll(cu, rows - 1, 0)[halo:halo + tm]
        w = cw_ref[...]
        y = gate_b * (before * w[0:1, :] + cu[halo:halo + tm] * w[1:2, :] + after * w[2:3, :])
        cgain = cg_ref[...]
        for gi in range(y.shape[-1] // HEAD_DIM):
            sl = slice(gi * HEAD_DIM, (gi + 1) * HEAD_DIM)
            conv_ref[:, sl] = _rms(y[:, sl], cgain[:, sl]).astype(BF16)

    @pl.when(j == 0)
    def _():
        g = g_ref[...]
        first = i % tiles_per_seq == 0
        last = i % tiles_per_seq == tiles_per_seq - 1
        hn_ref[0:halo, :] = jnp.where(first, 0.0, _rms(xp_ref[...], g)).astype(BF16)
        hn_ref[halo + tm:, :] = jnp.where(last, 0.0, _rms(xn_ref[...], g)).astype(BF16)
        for r0 in range(0, tm, NORM_CHUNK):
            hn = _rms(x_ref[r0:r0 + NORM_CHUNK, :], g).astype(BF16)
            hn_ref[halo + r0:halo + r0 + NORM_CHUNK, :] = hn
            head_norm(wq_ref, qg_ref, q_ref, hn, r0, NORM_CHUNK)
            head_norm(wk_ref, kg_ref, k_ref, hn, r0, NORM_CHUNK)
        conv_part()
        v_part(hn_ref[halo:halo + tm, :])

    @pl.when(j > 0)
    def _():
        conv_part()
        hn = hn_ref[halo:halo + tm, :]
        head_norm(wq_ref, qg_ref, q_ref, hn, 0, tm)
        head_norm(wk_ref, kg_ref, k_ref, hn, 0, tm)
        v_part(hn)


def _in_projection(x, norm1, w_in, w_vt, q_gain, k_gain, conv_w, conv_gain, layer, *,
                   seq, tm, tn):
    m, d = x.shape
    attn_w = N_HEADS * HEAD_DIM
    conv_width = conv_w.shape[-1]
    assert w_in.shape[-1] == 3 * attn_w + 3 * conv_width and attn_w == conv_width
    assert conv_width // CONV_GROUPS == HEAD_DIM and attn_w % tn == 0 and tn % HEAD_DIM == 0
    assert seq % tm == 0 and m % seq == 0 and tm % NORM_CHUNK == 0
    nj = attn_w // tn
    halo = BF16_ROWS
    per = tm // halo
    n_halo = m // halo
    wspec = lambda part: pl.BlockSpec((None, d, tn), lambda i, j: (layer, 0, part * nj + j))
    gainspec = pl.BlockSpec((None, 1, HEAD_DIM), lambda i, j: (layer, 0, 0))
    headspec = pl.BlockSpec((tn // HEAD_DIM, tm, HEAD_DIM), lambda i, j: (j, i, 0))
    heads = jax.ShapeDtypeStruct((N_HEADS, m, HEAD_DIM), BF16)
    vt_rows = HEAD_DIM + VT_PAD
    vt_shape = jax.ShapeDtypeStruct((N_HEADS, m // VT_CHUNK, vt_rows, VT_CHUNK), BF16)
    vt_spec = pl.BlockSpec((tn // HEAD_DIM, tm // VT_CHUNK, vt_rows, VT_CHUNK),
                           lambda i, j: (j, i, 0, 0))
    return pl.pallas_call(
        functools.partial(_inproj_kernel, tiles_per_seq=seq // tm),
        out_shape=(heads, heads, vt_shape, jax.ShapeDtypeStruct((m, conv_width), BF16)),
        grid=(m // tm, nj),
        in_specs=[
            pl.BlockSpec((tm, d), lambda i, j: (i, 0)),
            pl.BlockSpec((halo, d), lambda i, j: (jnp.maximum(i * per - 1, 0), 0)),
            pl.BlockSpec((halo, d), lambda i, j: (jnp.minimum((i + 1) * per, n_halo - 1), 0)),
            pl.BlockSpec((None, 1, d), lambda i, j: (layer, 0, 0)),
            wspec(0), wspec(1),
            pl.BlockSpec((None, tn, d), lambda i, j: (layer, j, 0)),
            wspec(3), wspec(4), wspec(5),
            gainspec, gainspec,
            pl.BlockSpec((None, CONV_K, tn), lambda i, j: (layer, 0, j)),
            pl.BlockSpec((None, 1, tn), lambda i, j: (layer, 0, j)),
        ],
        out_specs=(headspec, headspec, vt_spec, pl.BlockSpec((tm, tn), lambda i, j: (i, j))),
        scratch_shapes=[pltpu.VMEM((tm + 2 * halo, d), BF16)],
        compiler_params=_params("parallel", "arbitrary"),
        name="in_projection",
    )(x, x, x, norm1, w_in, w_in, w_vt, w_in, w_in, w_in, q_gain, k_gain, conv_w, conv_gain)


def _row_group_variants(n_rows):
    variants = []
    for kind in range(3):
        table = []
        for i in range(ROW_GROUP):
            row = []
            for j in range(KEY_ROWS):
                if kind == 0:
                    q_abs, k_abs = i, j
                elif kind == 1:
                    q_abs, k_abs = KEY_ROWS + i, KEY_ROWS - WIN_ROWS // 2 + j
                else:
                    q_abs, k_abs = n_rows - ROW_GROUP + i, n_rows - KEY_ROWS + j
                r0 = q_abs - WIN_ROWS // 2
                if kind != 1:
                    r0 = min(max(r0, 0), n_rows - WIN_ROWS)
                valid = r0 <= k_abs < r0 + WIN_ROWS
                row.append(k_abs - q_abs + WIN_ROWS - 1 if valid else None)
            table.append(row)
        variants.append(table)
    return variants


def _attn_kernel(q_ref, k_ref, vt_ref, t_ref, g_ref, o_ref, bias_ref, s_ref, acc_ref, *, n_rows):
    n_groups = n_rows // ROW_GROUP
    gq = ROW_GROUP * GRID_W
    gk = KEY_ROWS * GRID_W
    rows_per_chunk = VT_CHUNK // GRID_W

    left = lax.broadcasted_iota(jnp.int32, (GRID_W, 2 * GRID_W), 1) < GRID_W
    neg = jnp.full((GRID_W, 2 * GRID_W), NEG_INF, F32)
    for kind, table in enumerate(_row_group_variants(n_rows)):
        for j in range(KEY_ROWS):
            for ii in range(ROW_GROUP // 2):
                dl, dr = table[2 * ii][j], table[2 * ii + 1][j]
                if dl is None and dr is None:
                    blk = neg
                else:
                    lo = neg if dl is None else t_ref[dl]
                    hi = neg if dr is None else t_ref[dr]
                    blk = jnp.where(left, lo, hi)
                bias_ref[kind, j * GRID_W:(j + 1) * GRID_W,
                         ii * 2 * GRID_W:(ii + 1) * 2 * GRID_W] = blk

    gain = g_ref[...]

    def key_row0(g):
        return jnp.clip(g * ROW_GROUP - WIN_ROWS // 2, 0, n_rows - KEY_ROWS)

    def scores(g):
        kind = jnp.where(g == 0, 0, jnp.where(g == n_groups - 1, 2, 1))
        q = q_ref[pl.ds(pl.multiple_of(g * gq, gq), gq), :]
        k0 = pl.multiple_of(key_row0(g) * GRID_W, VT_CHUNK)
        s = jnp.concatenate([
            lax.dot_general(k_ref[pl.ds(k0 + c * VT_CHUNK, VT_CHUNK), :], q,
                            (((1,), (1,)), ((), ())), preferred_element_type=F32)
            for c in range(gk // VT_CHUNK)], axis=0)
        return s + bias_ref[kind]

    def attend(s, g):
        m = jnp.max(s, axis=0, keepdims=True)
        p = jnp.exp2(s - m).astype(BF16)
        c0 = key_row0(g) // rows_per_chunk
        o = None
        for c in range(gk // VT_CHUNK):
            part = jnp.dot(vt_ref[c0 + c], p[c * VT_CHUNK:(c + 1) * VT_CHUNK, :],
                           preferred_element_type=F32)
            o = part if o is None else o + part
        return o[0:HEAD_DIM] / o[HEAD_DIM:HEAD_DIM + 1]

    def emit(o, g):
        ms = jnp.mean(o * o, axis=0, keepdims=True)
        y = (o * lax.rsqrt(ms + EPS)).T * gain
        o_ref[pl.ds(pl.multiple_of(g * gq, gq), gq), :] = y.astype(o_ref.dtype)

    s_ref[...] = scores(0)
    acc_ref[...] = jnp.zeros_like(acc_ref)

    def body(t, carry):
        g0 = GROUPS_PER_ITER * t
        emit(acc_ref[...], jnp.maximum(g0 - 1, 0))
        s_cur = s_ref[...]
        for u in range(GROUPS_PER_ITER):
            g = g0 + u
            s_next = scores(jnp.minimum(g + 1, n_groups - 1))
            o = attend(s_cur, g)
            if u < GROUPS_PER_ITER - 1:
                emit(o, g)
            else:
                acc_ref[...] = o
            s_cur = s_next
        s_ref[...] = s_cur
        return carry

    lax.fori_loop(0, n_groups // GROUPS_PER_ITER, body, 0)
    emit(acc_ref[...], n_groups - 1)


def _attention(q, k, vt, bias_blocks, out_gain, layer, *, n_rows):
    _, m, _ = q.shape
    seq = n_rows * GRID_W
    b = m // seq
    gq, gk = ROW_GROUP * GRID_W, KEY_ROWS * GRID_W
    assert b * seq == m and n_rows % (GROUPS_PER_ITER * ROW_GROUP) == 0
    assert n_rows >= KEY_ROWS + ROW_GROUP and ROW_GROUP % 2 == 0
    assert seq % VT_CHUNK == 0 and gk % VT_CHUNK == 0 and gq % VT_CHUNK == 0
    assert (WIN_ROWS // 2 * GRID_W) % VT_CHUNK == 0
    blk = pl.BlockSpec((None, seq, HEAD_DIM), lambda h, bi: (h, bi, 0))
    return pl.pallas_call(
        functools.partial(_attn_kernel, n_rows=n_rows),
        out_shape=jax.ShapeDtypeStruct((N_HEADS, m, HEAD_DIM), BF16),
        grid=(N_HEADS, b),
        in_specs=[
            blk, blk,
            pl.BlockSpec((None, seq // VT_CHUNK, HEAD_DIM + VT_PAD, VT_CHUNK), lambda h, bi: (h, bi, 0, 0)),
            pl.BlockSpec((None, N_REL_ROWS, GRID_W, 2 * GRID_W), lambda h, bi: (h, 0, 0, 0)),
            pl.BlockSpec((None, 1, HEAD_DIM), lambda h, bi: (layer, 0, h)),
        ],
        out_specs=blk,
        scratch_shapes=[pltpu.VMEM((3, gk, gq), F32), pltpu.VMEM((gk, gq), F32),
                        pltpu.VMEM((HEAD_DIM, gq), F32)],
        compiler_params=_params("parallel", "parallel"),
        name="neighbourhood_attention",
    )(q, k, vt, bias_blocks, out_gain)


def _bias_blocks(rpb_layer):
    kc = np.arange(GRID_W)[:, None]
    c = np.arange(GRID_W)[None, :]
    start = np.clip(c - WIN_COLS // 2, 0, GRID_W - WIN_COLS)
    valid = (kc >= start) & (kc < start + WIN_COLS)
    rel = np.clip(kc - c + WIN_COLS - 1, 0, N_REL_COLS - 1)
    t = jnp.where(jnp.asarray(valid), rpb_layer.astype(F32)[:, :, rel] * LOG2_E, NEG_INF)
    return jnp.concatenate([t, t], axis=-1)


def _outproj_kernel(x_ref, a_ref, c_ref, wa_ref, wc_ref, o_ref):
    attn = jnp.concatenate([a_ref[h] for h in range(a_ref.shape[0])], axis=-1)
    acc = jnp.dot(attn, wa_ref[...], preferred_element_type=F32)
    acc = acc + jnp.dot(c_ref[...], wc_ref[...], preferred_element_type=F32)
    o_ref[...] = x_ref[...] + acc


def _out_projection(x, attn, conv, w_out, layer, *, tm):
    m, d = x.shape
    n_heads, _, head_dim = attn.shape
    aw, cw = n_heads * head_dim, conv.shape[-1]
    assert aw == cw and aw + cw == w_out.shape[1] and m % tm == 0
    return pl.pallas_call(
        _outproj_kernel,
        out_shape=jax.ShapeDtypeStruct((m, d), F32),
        grid=(m // tm,),
        in_specs=[
            pl.BlockSpec((tm, d), lambda i: (i, 0)),
            pl.BlockSpec((n_heads, tm, head_dim), lambda i: (0, i, 0)),
            pl.BlockSpec((tm, cw), lambda i: (i, 0)),
            pl.BlockSpec((None, aw, d), lambda i: (layer, 0, 0)),
            pl.BlockSpec((None, cw, d), lambda i: (layer, 1, 0)),
        ],
        out_specs=pl.BlockSpec((tm, d), lambda i: (i, 0)),
        compiler_params=_params("parallel"),
        name="out_projection",
    )(x, attn, conv, w_out, w_out)


def _ffn_kernel(x_ref, g_ref, wg_ref, wu_ref, wd_ref, o_ref, hn_ref):
    j = pl.program_id(1)
    tm = x_ref.shape[0]

    def swiglu(hn):
        gate = jnp.dot(hn, wg_ref[...], preferred_element_type=F32)
        up = jnp.dot(hn, wu_ref[...], preferred_element_type=F32)
        h = (gate * jax.nn.sigmoid(gate)) * up
        return jnp.dot(h.astype(BF16), wd_ref[...], preferred_element_type=F32)

    @pl.when(j == 0)
    def _():
        g = g_ref[...]
        for r0 in range(0, tm, NORM_CHUNK):
            rows = slice(r0, r0 + NORM_CHUNK)
            x = x_ref[rows, :]
            hn = _rms(x, g).astype(BF16)
            hn_ref[rows, :] = hn
            o_ref[rows, :] = x + swiglu(hn)

    @pl.when(j > 0)
    def _():
        o_ref[...] += swiglu(hn_ref[...])


def _feed_forward(x, norm2, w_gate, w_up, w_down, layer, *, tm, tf):
    m, d = x.shape
    f = w_gate.shape[-1]
    assert m % tm == 0 and f % tf == 0
    return pl.pallas_call(
        _ffn_kernel,
        out_shape=jax.ShapeDtypeStruct((m, d), F32),
        grid=(m // tm, f // tf),
        in_specs=[
            pl.BlockSpec((tm, d), lambda i, j: (i, 0)),
            pl.BlockSpec((None, 1, d), lambda i, j: (layer, 0, 0)),
            pl.BlockSpec((None, d, tf), lambda i, j: (layer, 0, j)),
            pl.BlockSpec((None, d, tf), lambda i, j: (layer, 0, j)),
            pl.BlockSpec((None, tf, d), lambda i, j: (layer, j, 0)),
        ],
        out_specs=pl.BlockSpec((tm, d), lambda i, j: (i, 0)),
        scratch_shapes=[pltpu.VMEM((tm, d), BF16)],
        compiler_params=_params("parallel", "arbitrary"),
        name="feed_forward",
    )(x, norm2, w_gate, w_up, w_down)


TILES = dict(
    inproj=dict(tm=1024, tn=256),
    outproj=dict(tm=512),
    ffn=dict(tm=1024, tf=512),
)


def kernel(x_prompt, x_sample, norm1, w_in, q_gain, k_gain, rpb, conv_w, attn_out_gain,
           conv_out_gain, w_out, norm2, w_gate, w_up, w_down):
    depth, d, _ = w_in.shape
    w_in_b, w_out_b, w_gate_b, w_up_b, w_down_b = (
        w.astype(BF16) for w in (w_in, w_out, w_gate, w_up, w_down))
    attn_w = N_HEADS * HEAD_DIM
    w_vt_b = jnp.swapaxes(w_in_b[:, :, 2 * attn_w:3 * attn_w], 1, 2)
    row = lambda a: a.reshape(depth, 1, a.shape[-1])
    norm1_r, norm2_r = row(norm1), row(norm2)
    ag_r, cg_r = row(attn_out_gain), row(conv_out_gain)
    qg_r = row(q_gain.astype(F32) * (HEAD_DIM ** -0.5 * LOG2_E))
    kg_r = row(k_gain.astype(F32))

    streams = []
    for x in (x_prompt, x_sample):
        b, seq, _ = x.shape
        streams.append([x.reshape(b * seq, d), b, seq])

    for layer in range(depth):
        bias_blocks = _bias_blocks(rpb[layer])
        for st in streams:
            x, b, seq = st
            q, k, vt, conv = _in_projection(x, norm1_r, w_in_b, w_vt_b, qg_r, kg_r, conv_w, cg_r,
                                            layer, seq=seq, **TILES["inproj"])
            attn = _attention(q, k, vt, bias_blocks, ag_r, layer, n_rows=seq // GRID_W)
            x = _out_projection(x, attn, conv, w_out_b, layer, **TILES["outproj"])
            st[0] = _feed_forward(x, norm2_r, w_gate_b, w_up_b, w_down_b, layer, **TILES["ffn"])

    return tuple(x.reshape(b, seq, d) for x, b, seq in streams)
```

```python
import functools
import math

import numpy as np
import jax
import jax.numpy as jnp
from jax import lax
from jax.experimental import pallas as pl
from jax.experimental.pallas import tpu as pltpu

GRID_W = 64
N_HEADS = 8
HEAD_DIM = 128
CONV_GROUPS = 8
CONV_K = 3
WIN_ROWS = 8
WIN_COLS = 16
EPS = 1e-6
NEG_INF = -1e30
LOG2_E = math.log2(math.e)

BF16_ROWS = 16
NORM_CHUNK = 256
VT_CHUNK = 256
VT_PAD = BF16_ROWS
VMEM_LIMIT = 56 * 1024 * 1024

ROW_GROUP = 4
KEY_ROWS = ROW_GROUP + WIN_ROWS
GROUPS_PER_ITER = 16
N_REL_ROWS = 2 * WIN_ROWS - 1
N_REL_COLS = 2 * WIN_COLS - 1

F32 = jnp.float32
BF16 = jnp.bfloat16


def _params(*sem):
    return pltpu.CompilerParams(dimension_semantics=sem, vmem_limit_bytes=VMEM_LIMIT)


def _rms(x, gain):
    ms = jnp.mean(x * x, axis=-1, keepdims=True)
    return (x * lax.rsqrt(ms + EPS)) * gain


def _inproj_kernel(x_ref, xp_ref, xn_ref, g_ref, wq_ref, wk_ref, wv_ref, wb_ref, wc_ref, wu_ref,
                   qg_ref, kg_ref, cw_ref, cg_ref, q_ref, k_ref, vt_ref, conv_ref, hn_ref,
                   *, tiles_per_seq):
    i = pl.program_id(0)
    j = pl.program_id(1)
    tm = x_ref.shape[0]
    halo = xp_ref.shape[0]

    def head_norm(w_ref, gain_ref, o_ref, hn, r0, nr):
        acc = jnp.dot(hn, w_ref[...], preferred_element_type=F32)
        gain = gain_ref[...]
        for hh in range(acc.shape[-1] // HEAD_DIM):
            a = acc[:, hh * HEAD_DIM:(hh + 1) * HEAD_DIM]
            o_ref[hh, r0:r0 + nr, :] = _rms(a, gain).astype(BF16)

    def v_part(hn):
        acc = lax.dot_general(wv_ref[...], hn, (((1,), (1,)), ((), ())),
                              preferred_element_type=F32)
        ones = jnp.ones((VT_PAD, VT_CHUNK), BF16)
        for hh in range(acc.shape[0] // HEAD_DIM):
            for c in range(acc.shape[1] // VT_CHUNK):
                blk = acc[hh * HEAD_DIM:(hh + 1) * HEAD_DIM, c * VT_CHUNK:(c + 1) * VT_CHUNK]
                vt_ref[hh, c, 0:HEAD_DIM, :] = blk.astype(BF16)
                vt_ref[hh, c, HEAD_DIM:, :] = ones

    def conv_part():
        hn_all = hn_ref[...]
        hn = hn_ref[halo:halo + tm, :]
        cu = (jnp.dot(hn_all, wc_ref[...], preferred_element_type=F32)
              * jnp.dot(hn_all, wu_ref[...], preferred_element_type=F32))
        gate_b = jnp.dot(hn, wb_ref[...], preferred_element_type=F32)
        rows = cu.shape[0]
        before = pltpu.roll(cu, 1, 0)[halo:halo + tm]
        after = pltpu.roll(cu, rows - 1, 0)[halo:halo + tm]
        w = cw_ref[...]
        y = gate_b * (before * w[0:1, :] + cu[halo:halo + tm] * w[1:2, :] + after * w[2:3, :])
        cgain = cg_ref[...]
        for gi in range(y.shape[-1] // HEAD_DIM):
            sl = slice(gi * HEAD_DIM, (gi + 1) * HEAD_DIM)
            conv_ref[:, sl] = _rms(y[:, sl], cgain[:, sl]).astype(BF16)

    @pl.when(j == 0)
    def _():
        g = g_ref[...]
        first = i % tiles_per_seq == 0
        last = i % tiles_per_seq == tiles_per_seq - 1
        hn_ref[0:halo, :] = jnp.where(first, 0.0, _rms(xp_ref[...], g)).astype(BF16)
        hn_ref[halo + tm:, :] = jnp.where(last, 0.0, _rms(xn_ref[...], g)).astype(BF16)
        for r0 in range(0, tm, NORM_CHUNK):
            hn = _rms(x_ref[r0:r0 + NORM_CHUNK, :], g).astype(BF16)
            hn_ref[halo + r0:halo + r0 + NORM_CHUNK, :] = hn
            head_norm(wq_ref, qg_ref, q_ref, hn, r0, NORM_CHUNK)
            head_norm(wk_ref, kg_ref, k_ref, hn, r0, NORM_CHUNK)
        conv_part()
        v_part(hn_ref[halo:halo + tm, :])

    @pl.when(j > 0)
    def _():
        conv_part()
        hn = hn_ref[halo:halo + tm, :]
        head_norm(wq_ref, qg_ref, q_ref, hn, 0, tm)
        head_norm(wk_ref, kg_ref, k_ref, hn, 0, tm)
        v_part(hn)


def _in_projection(x, norm1, w_in, w_vt, q_gain, k_gain, conv_w, conv_gain, layer, *,
                   seq, tm, tn):
    m, d = x.shape
    attn_w = N_HEADS * HEAD_DIM
    conv_width = conv_w.shape[-1]
    assert w_in.shape[-1] == 3 * attn_w + 3 * conv_width and attn_w == conv_width
    assert conv_width // CONV_GROUPS == HEAD_DIM and attn_w % tn == 0 and tn % HEAD_DIM == 0
    assert seq % tm == 0 and m % seq == 0 and tm % NORM_CHUNK == 0
    nj = attn_w // tn
    halo = BF16_ROWS
    per = tm // halo
    n_halo = m // halo
    wspec = lambda part: pl.BlockSpec((None, d, tn), lambda i, j: (layer, 0, part * nj + j))
    gainspec = pl.BlockSpec((None, 1, HEAD_DIM), lambda i, j: (layer, 0, 0))
    headspec = pl.BlockSpec((tn // HEAD_DIM, tm, HEAD_DIM), lambda i, j: (j, i, 0))
    heads = jax.ShapeDtypeStruct((N_HEADS, m, HEAD_DIM), BF16)
    vt_rows = HEAD_DIM + VT_PAD
    vt_shape = jax.ShapeDtypeStruct((N_HEADS, m // VT_CHUNK, vt_rows, VT_CHUNK), BF16)
    vt_spec = pl.BlockSpec((tn // HEAD_DIM, tm // VT_CHUNK, vt_rows, VT_CHUNK),
                           lambda i, j: (j, i, 0, 0))
    return pl.pallas_call(
        functools.partial(_inproj_kernel, tiles_per_seq=seq // tm),
        out_shape=(heads, heads, vt_shape, jax.ShapeDtypeStruct((m, conv_width), BF16)),
        grid=(m // tm, nj),
        in_specs=[
            pl.BlockSpec((tm, d), lambda i, j: (i, 0)),
            pl.BlockSpec((halo, d), lambda i, j: (jnp.maximum(i * per - 1, 0), 0)),
            pl.BlockSpec((halo, d), lambda i, j: (jnp.minimum((i + 1) * per, n_halo - 1), 0)),
            pl.BlockSpec((None, 1, d), lambda i, j: (layer, 0, 0)),
            wspec(0), wspec(1),
            pl.BlockSpec((None, tn, d), lambda i, j: (layer, j, 0)),
            wspec(3), wspec(4), wspec(5),
            gainspec, gainspec,
            pl.BlockSpec((None, CONV_K, tn), lambda i, j: (layer, 0, j)),
            pl.BlockSpec((None, 1, tn), lambda i, j: (layer, 0, j)),
        ],
        out_specs=(headspec, headspec, vt_spec, pl.BlockSpec((tm, tn), lambda i, j: (i, j))),
        scratch_shapes=[pltpu.VMEM((tm + 2 * halo, d), BF16)],
        compiler_params=_params("parallel", "arbitrary"),
        name="in_projection",
    )(x, x, x, norm1, w_in, w_in, w_vt, w_in, w_in, w_in, q_gain, k_gain, conv_w, conv_gain)


def _row_group_variants(n_rows):
    variants = []
    for kind in range(3):
        table = []
        for i in range(ROW_GROUP):
            row = []
            for j in range(KEY_ROWS):
                if kind == 0:
                    q_abs, k_abs = i, j
                elif kind == 1:
                    q_abs, k_abs = KEY_ROWS + i, KEY_ROWS - WIN_ROWS // 2 + j
                else:
                    q_abs, k_abs = n_rows - ROW_GROUP + i, n_rows - KEY_ROWS + j
                r0 = q_abs - WIN_ROWS // 2
                if kind != 1:
                    r0 = min(max(r0, 0), n_rows - WIN_ROWS)
                valid = r0 <= k_abs < r0 + WIN_ROWS
                row.append(k_abs - q_abs + WIN_ROWS - 1 if valid else None)
            table.append(row)
        variants.append(table)
    return variants


def _attn_kernel(q_ref, k_ref, vt_ref, t_ref, g_ref, o_ref, bias_ref, s_ref, acc_ref, *, n_rows):
    n_groups = n_rows // ROW_GROUP
    gq = ROW_GROUP * GRID_W
    gk = KEY_ROWS * GRID_W
    rows_per_chunk = VT_CHUNK // GRID_W

    left = lax.broadcasted_iota(jnp.int32, (GRID_W, 2 * GRID_W), 1) < GRID_W
    neg = jnp.full((GRID_W, 2 * GRID_W), NEG_INF, F32)
    for kind, table in enumerate(_row_group_variants(n_rows)):
        for j in range(KEY_ROWS):
            for ii in range(ROW_GROUP // 2):
                dl, dr = table[2 * ii][j], table[2 * ii + 1][j]
                if dl is None and dr is None:
                    blk = neg
                else:
                    lo = neg if dl is None else t_ref[dl]
                    hi = neg if dr is None else t_ref[dr]
                    blk = jnp.where(left, lo, hi)
                bias_ref[kind, j * GRID_W:(j + 1) * GRID_W,
                         ii * 2 * GRID_W:(ii + 1) * 2 * GRID_W] = blk

    gain = g_ref[...]

    def key_row0(g):
        return jnp.clip(g * ROW_GROUP - WIN_ROWS // 2, 0, n_rows - KEY_ROWS)

    def scores(g):
        kind = jnp.where(g == 0, 0, jnp.where(g == n_groups - 1, 2, 1))
        q = q_ref[pl.ds(pl.multiple_of(g * gq, gq), gq), :]
        k0 = pl.multiple_of(key_row0(g) * GRID_W, VT_CHUNK)
        s = jnp.concatenate([
            lax.dot_general(k_ref[pl.ds(k0 + c * VT_CHUNK, VT_CHUNK), :], q,
                            (((1,), (1,)), ((), ())), preferred_element_type=F32)
            for c in range(gk // VT_CHUNK)], axis=0)
        return s + bias_ref[kind]

    def attend(s, g):
        m = jnp.max(s, axis=0, keepdims=True)
        p = jnp.exp2(s - m).astype(BF16)
        c0 = key_row0(g) // rows_per_chunk
        o = None
        for c in range(gk // VT_CHUNK):
            part = jnp.dot(vt_ref[c0 + c], p[c * VT_CHUNK:(c + 1) * VT_CHUNK, :],
                           preferred_element_type=F32)
            o = part if o is None else o + part
        return o[0:HEAD_DIM] / o[HEAD_DIM:HEAD_DIM + 1]

    def emit(o, g):
        ms = jnp.mean(o * o, axis=0, keepdims=True)
        y = (o * lax.rsqrt(ms + EPS)).T * gain
        o_ref[pl.ds(pl.multiple_of(g * gq, gq), gq), :] = y.astype(o_ref.dtype)

    s_ref[...] = scores(0)
    acc_ref[...] = jnp.zeros_like(acc_ref)

    def body(t, carry):
        g0 = GROUPS_PER_ITER * t
        emit(acc_ref[...], jnp.maximum(g0 - 1, 0))
        s_cur = s_ref[...]
        for u in range(GROUPS_PER_ITER):
            g = g0 + u
            s_next = scores(jnp.minimum(g + 1, n_groups - 1))
            o = attend(s_cur, g)
            if u < GROUPS_PER_ITER - 1:
                emit(o, g)
            else:
                acc_ref[...] = o
            s_cur = s_next
        s_ref[...] = s_cur
        return carry

    lax.fori_loop(0, n_groups // GROUPS_PER_ITER, body, 0)
    emit(acc_ref[...], n_groups - 1)


def _attention(q, k, vt, bias_blocks, out_gain, layer, *, n_rows):
    _, m, _ = q.shape
    seq = n_rows * GRID_W
    b = m // seq
    gq, gk = ROW_GROUP * GRID_W, KEY_ROWS * GRID_W
    assert b * seq == m and n_rows % (GROUPS_PER_ITER * ROW_GROUP) == 0
    assert n_rows >= KEY_ROWS + ROW_GROUP and ROW_GROUP % 2 == 0
    assert seq % VT_CHUNK == 0 and gk % VT_CHUNK == 0 and gq % VT_CHUNK == 0
    assert (WIN_ROWS // 2 * GRID_W) % VT_CHUNK == 0
    blk = pl.BlockSpec((None, seq, HEAD_DIM), lambda h, bi: (h, bi, 0))
    return pl.pallas_call(
        functools.partial(_attn_kernel, n_rows=n_rows),
        out_shape=jax.ShapeDtypeStruct((N_HEADS, m, HEAD_DIM), BF16),
        grid=(N_HEADS, b),
        in_specs=[
            blk, blk,
            pl.BlockSpec((None, seq // VT_CHUNK, HEAD_DIM + VT_PAD, VT_CHUNK), lambda h, bi: (h, bi, 0, 0)),
            pl.BlockSpec((None, N_REL_ROWS, GRID_W, 2 * GRID_W), lambda h, bi: (h, 0, 0, 0)),
            pl.BlockSpec((None, 1, HEAD_DIM), lambda h, bi: (layer, 0, h)),
        ],
        out_specs=blk,
        scratch_shapes=[pltpu.VMEM((3, gk, gq), F32), pltpu.VMEM((gk, gq), F32),
                        pltpu.VMEM((HEAD_DIM, gq), F32)],
        compiler_params=_params("parallel", "parallel"),
        name="neighbourhood_attention",
    )(q, k, vt, bias_blocks, out_gain)


def _bias_blocks(rpb_layer):
    kc = np.arange(GRID_W)[:, None]
    c = np.arange(GRID_W)[None, :]
    start = np.clip(c - WIN_COLS // 2, 0, GRID_W - WIN_COLS)
    valid = (kc >= start) & (kc < start + WIN_COLS)
    select = (np.arange(N_REL_COLS)[:, None, None] == (kc - c + WIN_COLS - 1)[None]) & valid[None]
    t = jnp.einsum("hrd,dkc->hrkc", rpb_layer.astype(F32), jnp.asarray(select, F32),
                   precision=lax.Precision.HIGHEST)
    t = jnp.where(jnp.asarray(valid), t * LOG2_E, NEG_INF)
    return jnp.concatenate([t, t], axis=-1)


def _outproj_kernel(x_ref, a_ref, c_ref, wa_ref, wc_ref, o_ref):
    attn = jnp.concatenate([a_ref[h] for h in range(a_ref.shape[0])], axis=-1)
    acc = jnp.dot(attn, wa_ref[...], preferred_element_type=F32)
    acc = acc + jnp.dot(c_ref[...], wc_ref[...], preferred_element_type=F32)
    o_ref[...] = x_ref[...] + acc


def _out_projection(x, attn, conv, w_out, layer, *, tm):
    m, d = x.shape
    n_heads, _, head_dim = attn.shape
    aw, cw = n_heads * head_dim, conv.shape[-1]
    assert aw == cw and aw + cw == w_out.shape[1] and m % tm == 0
    return pl.pallas_call(
        _outproj_kernel,
        out_shape=jax.ShapeDtypeStruct((m, d), F32),
        grid=(m // tm,),
        in_specs=[
            pl.BlockSpec((tm, d), lambda i: (i, 0)),
            pl.BlockSpec((n_heads, tm, head_dim), lambda i: (0, i, 0)),
            pl.BlockSpec((tm, cw), lambda i: (i, 0)),
            pl.BlockSpec((None, aw, d), lambda i: (layer, 0, 0)),
            pl.BlockSpec((None, cw, d), lambda i: (layer, 1, 0)),
        ],
        out_specs=pl.BlockSpec((tm, d), lambda i: (i, 0)),
        compiler_params=_params("parallel"),
        name="out_projection",
    )(x, attn, conv, w_out, w_out)


def _ffn_kernel(x_ref, g_ref, wg_ref, wu_ref, wd_ref, o_ref, hn_ref):
    j = pl.program_id(1)
    tm = x_ref.shape[0]

    def swiglu(hn):
        gate = jnp.dot(hn, wg_ref[...], preferred_element_type=F32)
        up = jnp.dot(hn, wu_ref[...], preferred_element_type=F32)
        h = (gate * jax.nn.sigmoid(gate)) * up
        return jnp.dot(h.astype(BF16), wd_ref[...], preferred_element_type=F32)

    @pl.when(j == 0)
    def _():
        g = g_ref[...]
        for r0 in range(0, tm, NORM_CHUNK):
            rows = slice(r0, r0 + NORM_CHUNK)
            x = x_ref[rows, :]
            hn = _rms(x, g).astype(BF16)
            hn_ref[rows, :] = hn
            o_ref[rows, :] = x + swiglu(hn)

    @pl.when(j > 0)
    def _():
        o_ref[...] += swiglu(hn_ref[...])


def _feed_forward(x, norm2, w_gate, w_up, w_down, layer, *, tm, tf):
    m, d = x.shape
    f = w_gate.shape[-1]
    assert m % tm == 0 and f % tf == 0
    return pl.pallas_call(
        _ffn_kernel,
        out_shape=jax.ShapeDtypeStruct((m, d), F32),
        grid=(m // tm, f // tf),
        in_specs=[
            pl.BlockSpec((tm, d), lambda i, j: (i, 0)),
            pl.BlockSpec((None, 1, d), lambda i, j: (layer, 0, 0)),
            pl.BlockSpec((None, d, tf), lambda i, j: (layer, 0, j)),
            pl.BlockSpec((None, d, tf), lambda i, j: (layer, 0, j)),
            pl.BlockSpec((None, tf, d), lambda i, j: (layer, j, 0)),
        ],
        out_specs=pl.BlockSpec((tm, d), lambda i, j: (i, 0)),
        scratch_shapes=[pltpu.VMEM((tm, d), BF16)],
        compiler_params=_params("parallel", "arbitrary"),
        name="feed_forward",
    )(x, norm2, w_gate, w_up, w_down)


TILES = dict(
    inproj=dict(tm=1024, tn=256),
    outproj=dict(tm=512),
    ffn=dict(tm=1024, tf=512),
)


def kernel(x_prompt, x_sample, norm1, w_in, q_gain, k_gain, rpb, conv_w, attn_out_gain,
           conv_out_gain, w_out, norm2, w_gate, w_up, w_down):
    depth, d, _ = w_in.shape
    w_in_b, w_out_b, w_gate_b, w_up_b, w_down_b = (
        w.astype(BF16) for w in (w_in, w_out, w_gate, w_up, w_down))
    attn_w = N_HEADS * HEAD_DIM
    w_vt_b = jnp.swapaxes(w_in_b[:, :, 2 * attn_w:3 * attn_w], 1, 2)
    row = lambda a: a.reshape(depth, 1, a.shape[-1])
    norm1_r, norm2_r = row(norm1), row(norm2)
    ag_r, cg_r = row(attn_out_gain), row(conv_out_gain)
    qg_r = row(q_gain.astype(F32) * (HEAD_DIM ** -0.5 * LOG2_E))
    kg_r = row(k_gain.astype(F32))

    streams = []
    for x in (x_prompt, x_sample):
        b, seq, _ = x.shape
        streams.append([x.reshape(b * seq, d), b, seq])

    for layer in range(depth):
        bias_blocks = _bias_blocks(rpb[layer])
        for st in streams:
            x, b, seq = st
            q, k, vt, conv = _in_projection(x, norm1_r, w_in_b, w_vt_b, qg_r, kg_r, conv_w, cg_r,
                                            layer, seq=seq, **TILES["inproj"])
            attn = _attention(q, k, vt, bias_blocks, ag_r, layer, n_rows=seq // GRID_W)
            x = _out_projection(x, attn, conv, w_out_b, layer, **TILES["outproj"])
            st[0] = _feed_forward(x, norm2_r, w_gate_b, w_up_b, w_down_b, layer, **TILES["ffn"])

    return tuple(x.reshape(b, seq, d) for x, b, seq in streams)
```

```python
import functools
import math

import numpy as np
import jax
import jax.numpy as jnp
from jax import lax
from jax.experimental import pallas as pl
from jax.experimental.pallas import tpu as pltpu

GRID_W = 64
N_HEADS = 8
HEAD_DIM = 128
CONV_GROUPS = 8
CONV_K = 3
WIN_ROWS = 8
WIN_COLS = 16
EPS = 1e-6
NEG_INF = -1e30
LOG2_E = math.log2(math.e)

BF16_ROWS = 16
NORM_CHUNK = 256
VT_CHUNK = 256
VT_PAD = BF16_ROWS
VMEM_LIMIT = 56 * 1024 * 1024

ROW_GROUP = 4
KEY_ROWS = ROW_GROUP + WIN_ROWS
GROUPS_PER_ITER = 16
N_REL_ROWS = 2 * WIN_ROWS - 1
N_REL_COLS = 2 * WIN_COLS - 1

F32 = jnp.float32
BF16 = jnp.bfloat16


def _params(*sem):
    return pltpu.CompilerParams(dimension_semantics=sem, vmem_limit_bytes=VMEM_LIMIT)


def _rms(x, gain):
    ms = jnp.mean(x * x, axis=-1, keepdims=True)
    return (x * lax.rsqrt(ms + EPS)) * gain


def _inproj_kernel(x_ref, xp_ref, xn_ref, g_ref, wq_ref, wk_ref, wv_ref, wb_ref, wc_ref, wu_ref,
                   qg_ref, kg_ref, cw_ref, cg_ref, q_ref, k_ref, vt_ref, conv_ref, hn_ref,
                   *, tiles_per_seq):
    i = pl.program_id(0)
    j = pl.program_id(1)
    tm = x_ref.shape[0]
    halo = xp_ref.shape[0]

    def head_norm(w_ref, gain_ref, o_ref, hn, r0, nr):
        acc = jnp.dot(hn, w_ref[...], preferred_element_type=F32)
        gain = gain_ref[...]
        for hh in range(acc.shape[-1] // HEAD_DIM):
            a = acc[:, hh * HEAD_DIM:(hh + 1) * HEAD_DIM]
            o_ref[hh, r0:r0 + nr, :] = _rms(a, gain).astype(BF16)

    def v_part(hn):
        acc = lax.dot_general(wv_ref[...], hn, (((1,), (1,)), ((), ())),
                              preferred_element_type=F32)
        ones = jnp.ones((VT_PAD, VT_CHUNK), BF16)
        for hh in range(acc.shape[0] // HEAD_DIM):
            for c in range(acc.shape[1] // VT_CHUNK):
                blk = acc[hh * HEAD_DIM:(hh + 1) * HEAD_DIM, c * VT_CHUNK:(c + 1) * VT_CHUNK]
                vt_ref[hh, c, 0:HEAD_DIM, :] = blk.astype(BF16)
                vt_ref[hh, c, HEAD_DIM:, :] = ones

    def conv_part():
        hn_all = hn_ref[...]
        hn = hn_ref[halo:halo + tm, :]
        cu = (jnp.dot(hn_all, wc_ref[...], preferred_element_type=F32)
              * jnp.dot(hn_all, wu_ref[...], preferred_element_type=F32))
        gate_b = jnp.dot(hn, wb_ref[...], preferred_element_type=F32)
        rows = cu.shape[0]
        before = pltpu.roll(cu, 1, 0)[halo:halo + tm]
        after = pltpu.roll(cu, rows - 1, 0)[halo:halo + tm]
        w = cw_ref[...]
        y = gate_b * (before * w[0:1, :] + cu[halo:halo + tm] * w[1:2, :] + after * w[2:3, :])
        cgain = cg_ref[...]
        for gi in range(y.shape[-1] // HEAD_DIM):
            sl = slice(gi * HEAD_DIM, (gi + 1) * HEAD_DIM)
            conv_ref[:, sl] = _rms(y[:, sl], cgain[:, sl]).astype(BF16)

    @pl.when(j == 0)
    def _():
        g = g_ref[...]
        first = i % tiles_per_seq == 0
        last = i % tiles_per_seq == tiles_per_seq - 1
        hn_ref[0:halo, :] = jnp.where(first, 0.0, _rms(xp_ref[...], g)).astype(BF16)
        hn_ref[halo + tm:, :] = jnp.where(last, 0.0, _rms(xn_ref[...], g)).astype(BF16)
        for r0 in range(0, tm, NORM_CHUNK):
            hn = _rms(x_ref[r0:r0 + NORM_CHUNK, :], g).astype(BF16)
            hn_ref[halo + r0:halo + r0 + NORM_CHUNK, :] = hn
            head_norm(wq_ref, qg_ref, q_ref, hn, r0, NORM_CHUNK)
            head_norm(wk_ref, kg_ref, k_ref, hn, r0, NORM_CHUNK)
        conv_part()
        v_part(hn_ref[halo:halo + tm, :])

    @pl.when(j > 0)
    def _():
        conv_part()
        hn = hn_ref[halo:halo + tm, :]
        head_norm(wq_ref, qg_ref, q_ref, hn, 0, tm)
        head_norm(wk_ref, kg_ref, k_ref, hn, 0, tm)
        v_part(hn)


def _in_projection(x, norm1, w_in, w_vt, q_gain, k_gain, conv_w, conv_gain, layer, *,
                   seq, tm, tn):
    m, d = x.shape
    attn_w = N_HEADS * HEAD_DIM
    conv_width = conv_w.shape[-1]
    assert attn_w == conv_width
    assert conv_width // CONV_GROUPS == HEAD_DIM and attn_w % tn == 0 and tn % HEAD_DIM == 0
    assert seq % tm == 0 and m % seq == 0 and tm % NORM_CHUNK == 0
    nj = attn_w // tn
    halo = BF16_ROWS
    per = tm // halo
    n_halo = m // halo
    assert w_in.shape[1:] == (6 * nj, d, tn)
    wspec = lambda part: pl.BlockSpec((None, None, d, tn),
                                      lambda i, j: (layer, part * nj + j, 0, 0))
    gainspec = pl.BlockSpec((None, 1, HEAD_DIM), lambda i, j: (layer, 0, 0))
    headspec = pl.BlockSpec((tn // HEAD_DIM, tm, HEAD_DIM), lambda i, j: (j, i, 0))
    heads = jax.ShapeDtypeStruct((N_HEADS, m, HEAD_DIM), BF16)
    vt_rows = HEAD_DIM + VT_PAD
    vt_shape = jax.ShapeDtypeStruct((N_HEADS, m // VT_CHUNK, vt_rows, VT_CHUNK), BF16)
    vt_spec = pl.BlockSpec((tn // HEAD_DIM, tm // VT_CHUNK, vt_rows, VT_CHUNK),
                           lambda i, j: (j, i, 0, 0))
    return pl.pallas_call(
        functools.partial(_inproj_kernel, tiles_per_seq=seq // tm),
        out_shape=(heads, heads, vt_shape, jax.ShapeDtypeStruct((m, conv_width), BF16)),
        grid=(m // tm, nj),
        in_specs=[
            pl.BlockSpec((tm, d), lambda i, j: (i, 0)),
            pl.BlockSpec((halo, d), lambda i, j: (jnp.maximum(i * per - 1, 0), 0)),
            pl.BlockSpec((halo, d), lambda i, j: (jnp.minimum((i + 1) * per, n_halo - 1), 0)),
            pl.BlockSpec((None, 1, d), lambda i, j: (layer, 0, 0)),
            wspec(0), wspec(1),
            pl.BlockSpec((None, tn, d), lambda i, j: (layer, j, 0)),
            wspec(3), wspec(4), wspec(5),
            gainspec, gainspec,
            pl.BlockSpec((None, CONV_K, tn), lambda i, j: (layer, 0, j)),
            pl.BlockSpec((None, 1, tn), lambda i, j: (layer, 0, j)),
        ],
        out_specs=(headspec, headspec, vt_spec, pl.BlockSpec((tm, tn), lambda i, j: (i, j))),
        scratch_shapes=[pltpu.VMEM((tm + 2 * halo, d), BF16)],
        compiler_params=_params("parallel", "arbitrary"),
        name="in_projection",
    )(x, x, x, norm1, w_in, w_in, w_vt, w_in, w_in, w_in, q_gain, k_gain, conv_w, conv_gain)


def _row_group_variants(n_rows):
    variants = []
    for kind in range(3):
        table = []
        for i in range(ROW_GROUP):
            row = []
            for j in range(KEY_ROWS):
                if kind == 0:
                    q_abs, k_abs = i, j
                elif kind == 1:
                    q_abs, k_abs = KEY_ROWS + i, KEY_ROWS - WIN_ROWS // 2 + j
                else:
                    q_abs, k_abs = n_rows - ROW_GROUP + i, n_rows - KEY_ROWS + j
                r0 = q_abs - WIN_ROWS // 2
                if kind != 1:
                    r0 = min(max(r0, 0), n_rows - WIN_ROWS)
                valid = r0 <= k_abs < r0 + WIN_ROWS
                row.append(k_abs - q_abs + WIN_ROWS - 1 if valid else None)
            table.append(row)
        variants.append(table)
    return variants


def _attn_kernel(q_ref, k_ref, vt_ref, t_ref, g_ref, o_ref, bias_ref, s_ref, acc_ref, *, n_rows):
    n_groups = n_rows // ROW_GROUP
    gq = ROW_GROUP * GRID_W
    gk = KEY_ROWS * GRID_W
    rows_per_chunk = VT_CHUNK // GRID_W

    left = lax.broadcasted_iota(jnp.int32, (GRID_W, 2 * GRID_W), 1) < GRID_W
    neg = jnp.full((GRID_W, 2 * GRID_W), NEG_INF, F32)
    for kind, table in enumerate(_row_group_variants(n_rows)):
        for j in range(KEY_ROWS):
            for ii in range(ROW_GROUP // 2):
                dl, dr = table[2 * ii][j], table[2 * ii + 1][j]
                if dl is None and dr is None:
                    blk = neg
                else:
                    lo = neg if dl is None else t_ref[dl]
                    hi = neg if dr is None else t_ref[dr]
                    blk = jnp.where(left, lo, hi)
                bias_ref[kind, j * GRID_W:(j + 1) * GRID_W,
                         ii * 2 * GRID_W:(ii + 1) * 2 * GRID_W] = blk

    gain = g_ref[...]

    def key_row0(g):
        return jnp.clip(g * ROW_GROUP - WIN_ROWS // 2, 0, n_rows - KEY_ROWS)

    def scores(g):
        kind = jnp.where(g == 0, 0, jnp.where(g == n_groups - 1, 2, 1))
        q = q_ref[pl.ds(pl.multiple_of(g * gq, gq), gq), :]
        k0 = pl.multiple_of(key_row0(g) * GRID_W, VT_CHUNK)
        s = jnp.concatenate([
            lax.dot_general(k_ref[pl.ds(k0 + c * VT_CHUNK, VT_CHUNK), :], q,
                            (((1,), (1,)), ((), ())), preferred_element_type=F32)
            for c in range(gk // VT_CHUNK)], axis=0)
        return s + bias_ref[kind]

    def attend(s, g):
        m = jnp.max(s, axis=0, keepdims=True)
        p = jnp.exp2(s - m).astype(BF16)
        c0 = key_row0(g) // rows_per_chunk
        o = None
        for c in range(gk // VT_CHUNK):
            part = jnp.dot(vt_ref[c0 + c], p[c * VT_CHUNK:(c + 1) * VT_CHUNK, :],
                           preferred_element_type=F32)
            o = part if o is None else o + part
        return o[0:HEAD_DIM] / o[HEAD_DIM:HEAD_DIM + 1]

    def emit(o, g):
        ms = jnp.mean(o * o, axis=0, keepdims=True)
        y = (o * lax.rsqrt(ms + EPS)).T * gain
        o_ref[pl.ds(pl.multiple_of(g * gq, gq), gq), :] = y.astype(o_ref.dtype)

    s_ref[...] = scores(0)
    acc_ref[...] = jnp.zeros_like(acc_ref)

    def body(t, carry):
        g0 = GROUPS_PER_ITER * t
        emit(acc_ref[...], jnp.maximum(g0 - 1, 0))
        s_cur = s_ref[...]
        for u in range(GROUPS_PER_ITER):
            g = g0 + u
            s_next = scores(jnp.minimum(g + 1, n_groups - 1))
            o = attend(s_cur, g)
            if u < GROUPS_PER_ITER - 1:
                emit(o, g)
            else:
                acc_ref[...] = o
            s_cur = s_next
        s_ref[...] = s_cur
        return carry

    lax.fori_loop(0, n_groups // GROUPS_PER_ITER, body, 0)
    emit(acc_ref[...], n_groups - 1)


def _attention(q, k, vt, bias_blocks, out_gain, layer, *, n_rows):
    _, m, _ = q.shape
    seq = n_rows * GRID_W
    b = m // seq
    gq, gk = ROW_GROUP * GRID_W, KEY_ROWS * GRID_W
    assert b * seq == m and n_rows % (GROUPS_PER_ITER * ROW_GROUP) == 0
    assert n_rows >= KEY_ROWS + ROW_GROUP and ROW_GROUP % 2 == 0
    assert seq % VT_CHUNK == 0 and gk % VT_CHUNK == 0 and gq % VT_CHUNK == 0
    assert (WIN_ROWS // 2 * GRID_W) % VT_CHUNK == 0
    blk = pl.BlockSpec((None, seq, HEAD_DIM), lambda h, bi: (h, bi, 0))
    return pl.pallas_call(
        functools.partial(_attn_kernel, n_rows=n_rows),
        out_shape=jax.ShapeDtypeStruct((N_HEADS, m, HEAD_DIM), BF16),
        grid=(N_HEADS, b),
        in_specs=[
            blk, blk,
            pl.BlockSpec((None, seq // VT_CHUNK, HEAD_DIM + VT_PAD, VT_CHUNK), lambda h, bi: (h, bi, 0, 0)),
            pl.BlockSpec((None, N_REL_ROWS, GRID_W, 2 * GRID_W), lambda h, bi: (h, 0, 0, 0)),
            pl.BlockSpec((None, 1, HEAD_DIM), lambda h, bi: (layer, 0, h)),
        ],
        out_specs=blk,
        scratch_shapes=[pltpu.VMEM((3, gk, gq), F32), pltpu.VMEM((gk, gq), F32),
                        pltpu.VMEM((HEAD_DIM, gq), F32)],
        compiler_params=_params("parallel", "parallel"),
        name="neighbourhood_attention",
    )(q, k, vt, bias_blocks, out_gain)


def _bias_blocks(rpb_layer):
    kc = np.arange(GRID_W)[:, None]
    c = np.arange(GRID_W)[None, :]
    start = np.clip(c - WIN_COLS // 2, 0, GRID_W - WIN_COLS)
    valid = (kc >= start) & (kc < start + WIN_COLS)
    select = (np.arange(N_REL_COLS)[:, None, None] == (kc - c + WIN_COLS - 1)[None]) & valid[None]
    t = jnp.einsum("hrd,dkc->hrkc", rpb_layer.astype(F32), jnp.asarray(select, F32),
                   precision=lax.Precision.HIGHEST)
    t = jnp.where(jnp.asarray(valid), t * LOG2_E, NEG_INF)
    return jnp.concatenate([t, t], axis=-1)


def _outproj_kernel(x_ref, a_ref, c_ref, wa_ref, wc_ref, o_ref):
    attn = jnp.concatenate([a_ref[h] for h in range(a_ref.shape[0])], axis=-1)
    acc = jnp.dot(attn, wa_ref[...], preferred_element_type=F32)
    acc = acc + jnp.dot(c_ref[...], wc_ref[...], preferred_element_type=F32)
    o_ref[...] = x_ref[...] + acc


def _out_projection(x, attn, conv, w_out, layer, *, tm):
    m, d = x.shape
    n_heads, _, head_dim = attn.shape
    aw, cw = n_heads * head_dim, conv.shape[-1]
    assert aw == cw and aw + cw == w_out.shape[1] and m % tm == 0
    return pl.pallas_call(
        _outproj_kernel,
        out_shape=jax.ShapeDtypeStruct((m, d), F32),
        grid=(m // tm,),
        in_specs=[
            pl.BlockSpec((tm, d), lambda i: (i, 0)),
            pl.BlockSpec((n_heads, tm, head_dim), lambda i: (0, i, 0)),
            pl.BlockSpec((tm, cw), lambda i: (i, 0)),
            pl.BlockSpec((None, aw, d), lambda i: (layer, 0, 0)),
            pl.BlockSpec((None, cw, d), lambda i: (layer, 1, 0)),
        ],
        out_specs=pl.BlockSpec((tm, d), lambda i: (i, 0)),
        compiler_params=_params("parallel"),
        name="out_projection",
    )(x, attn, conv, w_out, w_out)


def _ffn_kernel(x_ref, g_ref, wg_ref, wu_ref, wd_ref, o_ref, hn_ref):
    j = pl.program_id(1)
    tm = x_ref.shape[0]

    def swiglu(hn):
        gate = jnp.dot(hn, wg_ref[...], preferred_element_type=F32)
        up = jnp.dot(hn, wu_ref[...], preferred_element_type=F32)
        h = (gate * jax.nn.sigmoid(gate)) * up
        return jnp.dot(h.astype(BF16), wd_ref[...], preferred_element_type=F32)

    @pl.when(j == 0)
    def _():
        g = g_ref[...]
        for r0 in range(0, tm, NORM_CHUNK):
            rows = slice(r0, r0 + NORM_CHUNK)
            x = x_ref[rows, :]
            hn = _rms(x, g).astype(BF16)
            hn_ref[rows, :] = hn
            o_ref[rows, :] = x + swiglu(hn)

    @pl.when(j > 0)
    def _():
        o_ref[...] += swiglu(hn_ref[...])


def _feed_forward(x, norm2, w_gate, w_up, w_down, layer, *, tm, tf):
    m, d = x.shape
    f = w_down.shape[1]
    assert m % tm == 0 and f % tf == 0 and w_gate.shape[1:] == (f // tf, d, tf)
    return pl.pallas_call(
        _ffn_kernel,
        out_shape=jax.ShapeDtypeStruct((m, d), F32),
        grid=(m // tm, f // tf),
        in_specs=[
            pl.BlockSpec((tm, d), lambda i, j: (i, 0)),
            pl.BlockSpec((None, 1, d), lambda i, j: (layer, 0, 0)),
            pl.BlockSpec((None, None, d, tf), lambda i, j: (layer, j, 0, 0)),
            pl.BlockSpec((None, None, d, tf), lambda i, j: (layer, j, 0, 0)),
            pl.BlockSpec((None, tf, d), lambda i, j: (layer, j, 0)),
        ],
        out_specs=pl.BlockSpec((tm, d), lambda i, j: (i, 0)),
        scratch_shapes=[pltpu.VMEM((tm, d), BF16)],
        compiler_params=_params("parallel", "arbitrary"),
        name="feed_forward",
    )(x, norm2, w_gate, w_up, w_down)


TILES = dict(
    inproj=dict(tm=1024, tn=256),
    outproj=dict(tm=512),
    ffn=dict(tm=1024, tf=512),
)


def _column_blocks(w, width):
    depth, k, n = w.shape
    return w.reshape(depth, k, n // width, width).transpose(0, 2, 1, 3)


def kernel(x_prompt, x_sample, norm1, w_in, q_gain, k_gain, rpb, conv_w, attn_out_gain,
           conv_out_gain, w_out, norm2, w_gate, w_up, w_down):
    depth, d, _ = w_in.shape
    attn_w = N_HEADS * HEAD_DIM
    w_out_b, w_down_b = w_out.astype(BF16), w_down.astype(BF16)
    w_in_b = _column_blocks(w_in.astype(BF16), TILES["inproj"]["tn"])
    w_gate_b = _column_blocks(w_gate.astype(BF16), TILES["ffn"]["tf"])
    w_up_b = _column_blocks(w_up.astype(BF16), TILES["ffn"]["tf"])
    w_vt_b = jnp.swapaxes(w_in[:, :, 2 * attn_w:3 * attn_w].astype(BF16), 1, 2)
    row = lambda a: a.reshape(depth, 1, a.shape[-1])
    norm1_r, norm2_r = row(norm1), row(norm2)
    ag_r, cg_r = row(attn_out_gain), row(conv_out_gain)
    qg_r = row(q_gain.astype(F32) * (HEAD_DIM ** -0.5 * LOG2_E))
    kg_r = row(k_gain.astype(F32))

    streams = []
    for x in (x_prompt, x_sample):
        b, seq, _ = x.shape
        streams.append([x.reshape(b * seq, d), b, seq])

    for layer in range(depth):
        bias_blocks = _bias_blocks(rpb[layer])
        for st in streams:
            x, b, seq = st
            q, k, vt, conv = _in_projection(x, norm1_r, w_in_b, w_vt_b, qg_r, kg_r, conv_w, cg_r,
                                            layer, seq=seq, **TILES["inproj"])
            attn = _attention(q, k, vt, bias_blocks, ag_r, layer, n_rows=seq // GRID_W)
            x = _out_projection(x, attn, conv, w_out_b, layer, **TILES["outproj"])
            st[0] = _feed_forward(x, norm2_r, w_gate_b, w_up_b, w_down_b, layer, **TILES["ffn"])

    return tuple(x.reshape(b, seq, d) for x, b, seq in streams)
```

```python
import functools
import math

import numpy as np
import jax
import jax.numpy as jnp
from jax import lax
from jax.experimental import pallas as pl
from jax.experimental.pallas import tpu as pltpu

GRID_W = 64
N_HEADS = 8
HEAD_DIM = 128
CONV_GROUPS = 8
CONV_K = 3
WIN_ROWS = 8
WIN_COLS = 16
EPS = 1e-6
NEG_INF = -1e30
LOG2_E = math.log2(math.e)

BF16_ROWS = 16
NORM_CHUNK = 256
FFN_NORM_CHUNK = 512
VT_CHUNK = 256
VT_PAD = BF16_ROWS
VMEM_LIMIT = 56 * 1024 * 1024

ROW_GROUP = 4
KEY_ROWS = ROW_GROUP + WIN_ROWS
GROUPS_PER_ITER = 32
N_REL_ROWS = 2 * WIN_ROWS - 1
N_REL_COLS = 2 * WIN_COLS - 1

F32 = jnp.float32
BF16 = jnp.bfloat16


def _params(*sem):
    return pltpu.CompilerParams(dimension_semantics=sem, vmem_limit_bytes=VMEM_LIMIT)


def _rms(x, gain):
    ms = jnp.mean(x * x, axis=-1, keepdims=True)
    return (x * lax.rsqrt(ms + EPS)) * gain


def _inproj_kernel(x_ref, xp_ref, xn_ref, g_ref, wq_ref, wk_ref, wv_ref, wb_ref, wc_ref, wu_ref,
                   qg_ref, kg_ref, cw_ref, cg_ref, q_ref, k_ref, vt_ref, conv_ref, hn_ref,
                   *, tiles_per_seq):
    i = pl.program_id(0)
    j = pl.program_id(1)
    tm = x_ref.shape[0]
    halo = xp_ref.shape[0]

    def head_norm(w_ref, gain_ref, o_ref, hn, r0, nr):
        acc = jnp.dot(hn, w_ref[...], preferred_element_type=F32)
        gain = gain_ref[...]
        for hh in range(acc.shape[-1] // HEAD_DIM):
            a = acc[:, hh * HEAD_DIM:(hh + 1) * HEAD_DIM]
            o_ref[hh, r0:r0 + nr, :] = _rms(a, gain).astype(BF16)

    def v_part(hn):
        acc = lax.dot_general(wv_ref[...], hn, (((1,), (1,)), ((), ())),
                              preferred_element_type=F32)
        ones = jnp.ones((VT_PAD, VT_CHUNK), BF16)
        for hh in range(acc.shape[0] // HEAD_DIM):
            for c in range(acc.shape[1] // VT_CHUNK):
                blk = acc[hh * HEAD_DIM:(hh + 1) * HEAD_DIM, c * VT_CHUNK:(c + 1) * VT_CHUNK]
                vt_ref[hh, c, 0:HEAD_DIM, :] = blk.astype(BF16)
                vt_ref[hh, c, HEAD_DIM:, :] = ones

    def conv_part():
        hn_all = hn_ref[...]
        hn = hn_ref[halo:halo + tm, :]
        cu = (jnp.dot(hn_all, wc_ref[...], preferred_element_type=F32)
              * jnp.dot(hn_all, wu_ref[...], preferred_element_type=F32))
        gate_b = jnp.dot(hn, wb_ref[...], preferred_element_type=F32)
        rows = cu.shape[0]
        before = pltpu.roll(cu, 1, 0)[halo:halo + tm]
        after = pltpu.roll(cu, rows - 1, 0)[halo:halo + tm]
        w = cw_ref[...]
        y = gate_b * (before * w[0:1, :] + cu[halo:halo + tm] * w[1:2, :] + after * w[2:3, :])
        cgain = cg_ref[...]
        for gi in range(y.shape[-1] // HEAD_DIM):
            sl = slice(gi * HEAD_DIM, (gi + 1) * HEAD_DIM)
            conv_ref[:, sl] = _rms(y[:, sl], cgain[:, sl]).astype(BF16)

    @pl.when(j == 0)
    def _():
        g = g_ref[...]
        first = i % tiles_per_seq == 0
        last = i % tiles_per_seq == tiles_per_seq - 1
        hn_ref[0:halo, :] = jnp.where(first, 0.0, _rms(xp_ref[...], g)).astype(BF16)
        hn_ref[halo + tm:, :] = jnp.where(last, 0.0, _rms(xn_ref[...], g)).astype(BF16)
        for r0 in range(0, tm, NORM_CHUNK):
            hn = _rms(x_ref[r0:r0 + NORM_CHUNK, :], g).astype(BF16)
            hn_ref[halo + r0:halo + r0 + NORM_CHUNK, :] = hn
            head_norm(wq_ref, qg_ref, q_ref, hn, r0, NORM_CHUNK)
            head_norm(wk_ref, kg_ref, k_ref, hn, r0, NORM_CHUNK)
        conv_part()
        v_part(hn_ref[halo:halo + tm, :])

    @pl.when(j > 0)
    def _():
        conv_part()
        hn = hn_ref[halo:halo + tm, :]
        head_norm(wq_ref, qg_ref, q_ref, hn, 0, tm)
        head_norm(wk_ref, kg_ref, k_ref, hn, 0, tm)
        v_part(hn)


def _in_projection(x, norm1, w_in, w_vt, q_gain, k_gain, conv_w, conv_gain, layer, *,
                   seq, tm, tn):
    m, d = x.shape
    attn_w = N_HEADS * HEAD_DIM
    conv_width = conv_w.shape[-1]
    assert w_in.shape[-1] == 3 * attn_w + 3 * conv_width and attn_w == conv_width
    assert conv_width // CONV_GROUPS == HEAD_DIM and attn_w % tn == 0 and tn % HEAD_DIM == 0
    assert seq % tm == 0 and m % seq == 0 and tm % NORM_CHUNK == 0
    nj = attn_w // tn
    halo = BF16_ROWS
    per = tm // halo
    n_halo = m // halo
    wspec = lambda part: pl.BlockSpec((None, d, tn), lambda i, j: (layer, 0, part * nj + j))
    gainspec = pl.BlockSpec((None, 1, HEAD_DIM), lambda i, j: (layer, 0, 0))
    headspec = pl.BlockSpec((tn // HEAD_DIM, tm, HEAD_DIM), lambda i, j: (j, i, 0))
    heads = jax.ShapeDtypeStruct((N_HEADS, m, HEAD_DIM), BF16)
    vt_rows = HEAD_DIM + VT_PAD
    vt_shape = jax.ShapeDtypeStruct((N_HEADS, m // VT_CHUNK, vt_rows, VT_CHUNK), BF16)
    vt_spec = pl.BlockSpec((tn // HEAD_DIM, tm // VT_CHUNK, vt_rows, VT_CHUNK),
                           lambda i, j: (j, i, 0, 0))
    return pl.pallas_call(
        functools.partial(_inproj_kernel, tiles_per_seq=seq // tm),
        out_shape=(heads, heads, vt_shape, jax.ShapeDtypeStruct((m, conv_width), BF16)),
        grid=(m // tm, nj),
        in_specs=[
            pl.BlockSpec((tm, d), lambda i, j: (i, 0)),
            pl.BlockSpec((halo, d), lambda i, j: (jnp.maximum(i * per - 1, 0), 0)),
            pl.BlockSpec((halo, d), lambda i, j: (jnp.minimum((i + 1) * per, n_halo - 1), 0)),
            pl.BlockSpec((None, 1, d), lambda i, j: (layer, 0, 0)),
            wspec(0), wspec(1),
            pl.BlockSpec((None, tn, d), lambda i, j: (layer, j, 0)),
            wspec(3), wspec(4), wspec(5),
            gainspec, gainspec,
            pl.BlockSpec((None, CONV_K, tn), lambda i, j: (layer, 0, j)),
            pl.BlockSpec((None, 1, tn), lambda i, j: (layer, 0, j)),
        ],
        out_specs=(headspec, headspec, vt_spec, pl.BlockSpec((tm, tn), lambda i, j: (i, j))),
        scratch_shapes=[pltpu.VMEM((tm + 2 * halo, d), BF16)],
        compiler_params=_params("parallel", "arbitrary"),
        name="in_projection",
    )(x, x, x, norm1, w_in, w_in, w_vt, w_in, w_in, w_in, q_gain, k_gain, conv_w, conv_gain)


def _row_group_variants(n_rows):
    variants = []
    for kind in range(3):
        table = []
        for i in range(ROW_GROUP):
            row = []
            for j in range(KEY_ROWS):
                if kind == 0:
                    q_abs, k_abs = i, j
                elif kind == 1:
                    q_abs, k_abs = KEY_ROWS + i, KEY_ROWS - WIN_ROWS // 2 + j
                else:
                    q_abs, k_abs = n_rows - ROW_GROUP + i, n_rows - KEY_ROWS + j
                r0 = q_abs - WIN_ROWS // 2
                if kind != 1:
                    r0 = min(max(r0, 0), n_rows - WIN_ROWS)
                valid = r0 <= k_abs < r0 + WIN_ROWS
                row.append(k_abs - q_abs + WIN_ROWS - 1 if valid else None)
            table.append(row)
        variants.append(table)
    return variants


def _attn_kernel(q_ref, k_ref, vt_ref, t_ref, g_ref, o_ref, bias_ref, s_ref, acc_ref, *, n_rows):
    n_groups = n_rows // ROW_GROUP
    gq = ROW_GROUP * GRID_W
    gk = KEY_ROWS * GRID_W
    rows_per_chunk = VT_CHUNK // GRID_W

    left = lax.broadcasted_iota(jnp.int32, (GRID_W, 2 * GRID_W), 1) < GRID_W
    neg = jnp.full((GRID_W, 2 * GRID_W), NEG_INF, F32)
    for kind, table in enumerate(_row_group_variants(n_rows)):
        for j in range(KEY_ROWS):
            for ii in range(ROW_GROUP // 2):
                dl, dr = table[2 * ii][j], table[2 * ii + 1][j]
                if dl is None and dr is None:
                    blk = neg
                else:
                    lo = neg if dl is None else t_ref[dl]
                    hi = neg if dr is None else t_ref[dr]
                    blk = jnp.where(left, lo, hi)
                bias_ref[kind, j * GRID_W:(j + 1) * GRID_W,
                         ii * 2 * GRID_W:(ii + 1) * 2 * GRID_W] = blk

    gain = g_ref[...]

    def key_row0(g):
        return jnp.clip(g * ROW_GROUP - WIN_ROWS // 2, 0, n_rows - KEY_ROWS)

    def scores(g):
        kind = jnp.where(g == 0, 0, jnp.where(g == n_groups - 1, 2, 1))
        q = q_ref[pl.ds(pl.multiple_of(g * gq, gq), gq), :]
        k0 = pl.multiple_of(key_row0(g) * GRID_W, VT_CHUNK)
        s = jnp.concatenate([
            lax.dot_general(k_ref[pl.ds(k0 + c * VT_CHUNK, VT_CHUNK), :], q,
                            (((1,), (1,)), ((), ())), preferred_element_type=F32)
            for c in range(gk // VT_CHUNK)], axis=0)
        return s + bias_ref[kind]

    def attend(s, g):
        m = jnp.max(s, axis=0, keepdims=True)
        p = jnp.exp2(s - m).astype(BF16)
        c0 = key_row0(g) // rows_per_chunk
        o = None
        for c in range(gk // VT_CHUNK):
            part = jnp.dot(vt_ref[c0 + c], p[c * VT_CHUNK:(c + 1) * VT_CHUNK, :],
                           preferred_element_type=F32)
            o = part if o is None else o + part
        return o[0:HEAD_DIM] / o[HEAD_DIM:HEAD_DIM + 1]

    def emit(o, g):
        ms = jnp.mean(o * o, axis=0, keepdims=True)
        y = (o * lax.rsqrt(ms + EPS)).T * gain
        o_ref[pl.ds(pl.multiple_of(g * gq, gq), gq), :] = y.astype(o_ref.dtype)

    s_ref[...] = scores(0)
    acc_ref[...] = jnp.zeros_like(acc_ref)

    def body(t, carry):
        g0 = GROUPS_PER_ITER * t
        emit(acc_ref[...], jnp.maximum(g0 - 1, 0))
        s_cur = s_ref[...]
        for u in range(GROUPS_PER_ITER):
            g = g0 + u
            s_next = scores(jnp.minimum(g + 1, n_groups - 1))
            o = attend(s_cur, g)
            if u < GROUPS_PER_ITER - 1:
                emit(o, g)
            else:
                acc_ref[...] = o
            s_cur = s_next
        s_ref[...] = s_cur
        return carry

    lax.fori_loop(0, n_groups // GROUPS_PER_ITER, body, 0)
    emit(acc_ref[...], n_groups - 1)


def _attention(q, k, vt, bias_blocks, out_gain, layer, *, n_rows):
    _, m, _ = q.shape
    seq = n_rows * GRID_W
    b = m // seq
    gq, gk = ROW_GROUP * GRID_W, KEY_ROWS * GRID_W
    assert b * seq == m and n_rows % (GROUPS_PER_ITER * ROW_GROUP) == 0
    assert n_rows >= KEY_ROWS + ROW_GROUP and ROW_GROUP % 2 == 0
    assert seq % VT_CHUNK == 0 and gk % VT_CHUNK == 0 and gq % VT_CHUNK == 0
    assert (WIN_ROWS // 2 * GRID_W) % VT_CHUNK == 0
    blk = pl.BlockSpec((None, seq, HEAD_DIM), lambda h, bi: (h, bi, 0))
    return pl.pallas_call(
        functools.partial(_attn_kernel, n_rows=n_rows),
        out_shape=jax.ShapeDtypeStruct((N_HEADS, m, HEAD_DIM), BF16),
        grid=(N_HEADS, b),
        in_specs=[
            blk, blk,
            pl.BlockSpec((None, seq // VT_CHUNK, HEAD_DIM + VT_PAD, VT_CHUNK), lambda h, bi: (h, bi, 0, 0)),
            pl.BlockSpec((None, N_REL_ROWS, GRID_W, 2 * GRID_W), lambda h, bi: (h, 0, 0, 0)),
            pl.BlockSpec((None, 1, HEAD_DIM), lambda h, bi: (layer, 0, h)),
        ],
        out_specs=blk,
        scratch_shapes=[pltpu.VMEM((3, gk, gq), F32), pltpu.VMEM((gk, gq), F32),
                        pltpu.VMEM((HEAD_DIM, gq), F32)],
        compiler_params=_params("parallel", "parallel"),
        name="neighbourhood_attention",
    )(q, k, vt, bias_blocks, out_gain)


def _bias_blocks(rpb_layer):
    kc = np.arange(GRID_W)[:, None]
    c = np.arange(GRID_W)[None, :]
    start = np.clip(c - WIN_COLS // 2, 0, GRID_W - WIN_COLS)
    valid = (kc >= start) & (kc < start + WIN_COLS)
    select = (np.arange(N_REL_COLS)[:, None, None] == (kc - c + WIN_COLS - 1)[None]) & valid[None]
    t = jnp.einsum("hrd,dkc->hrkc", rpb_layer.astype(F32), jnp.asarray(select, F32),
                   precision=lax.Precision.HIGHEST)
    t = jnp.where(jnp.asarray(valid), t * LOG2_E, NEG_INF)
    return jnp.concatenate([t, t], axis=-1)


def _outproj_kernel(x_ref, a_ref, c_ref, wa_ref, wc_ref, o_ref):
    attn = jnp.concatenate([a_ref[h] for h in range(a_ref.shape[0])], axis=-1)
    acc = jnp.dot(attn, wa_ref[...], preferred_element_type=F32)
    acc = acc + jnp.dot(c_ref[...], wc_ref[...], preferred_element_type=F32)
    o_ref[...] = x_ref[...] + acc


def _out_projection(x, attn, conv, w_out, layer, *, tm):
    m, d = x.shape
    n_heads, _, head_dim = attn.shape
    aw, cw = n_heads * head_dim, conv.shape[-1]
    assert aw == cw and aw + cw == w_out.shape[1] and m % tm == 0
    return pl.pallas_call(
        _outproj_kernel,
        out_shape=jax.ShapeDtypeStruct((m, d), F32),
        grid=(m // tm,),
        in_specs=[
            pl.BlockSpec((tm, d), lambda i: (i, 0)),
            pl.BlockSpec((n_heads, tm, head_dim), lambda i: (0, i, 0)),
            pl.BlockSpec((tm, cw), lambda i: (i, 0)),
            pl.BlockSpec((None, aw, d), lambda i: (layer, 0, 0)),
            pl.BlockSpec((None, cw, d), lambda i: (layer, 1, 0)),
        ],
        out_specs=pl.BlockSpec((tm, d), lambda i: (i, 0)),
        compiler_params=_params("parallel"),
        name="out_projection",
    )(x, attn, conv, w_out, w_out)


def _ffn_kernel(x_ref, g_ref, wg_ref, wu_ref, wd_ref, o_ref, hn_ref):
    j = pl.program_id(1)
    tm = x_ref.shape[0]

    def swiglu(hn):
        gate = jnp.dot(hn, wg_ref[...], preferred_element_type=F32)
        up = jnp.dot(hn, wu_ref[...], preferred_element_type=F32)
        h = (gate * jax.nn.sigmoid(gate)) * up
        return jnp.dot(h.astype(BF16), wd_ref[...], preferred_element_type=F32)

    @pl.when(j == 0)
    def _():
        g = g_ref[...]
        for r0 in range(0, tm, FFN_NORM_CHUNK):
            rows = slice(r0, r0 + FFN_NORM_CHUNK)
            x = x_ref[rows, :]
            hn = _rms(x, g).astype(BF16)
            hn_ref[rows, :] = hn
            o_ref[rows, :] = x + swiglu(hn)

    @pl.when(j > 0)
    def _():
        o_ref[...] += swiglu(hn_ref[...])


def _feed_forward(x, norm2, w_gate, w_up, w_down, layer, *, tm, tf):
    m, d = x.shape
    f = w_gate.shape[-1]
    assert m % tm == 0 and f % tf == 0 and tm % FFN_NORM_CHUNK == 0
    return pl.pallas_call(
        _ffn_kernel,
        out_shape=jax.ShapeDtypeStruct((m, d), F32),
        grid=(m // tm, f // tf),
        in_specs=[
            pl.BlockSpec((tm, d), lambda i, j: (i, 0)),
            pl.BlockSpec((None, 1, d), lambda i, j: (layer, 0, 0)),
            pl.BlockSpec((None, d, tf), lambda i, j: (layer, 0, j)),
            pl.BlockSpec((None, d, tf), lambda i, j: (layer, 0, j)),
            pl.BlockSpec((None, tf, d), lambda i, j: (layer, j, 0)),
        ],
        out_specs=pl.BlockSpec((tm, d), lambda i, j: (i, 0)),
        scratch_shapes=[pltpu.VMEM((tm, d), BF16)],
        compiler_params=_params("parallel", "arbitrary"),
        name="feed_forward",
    )(x, norm2, w_gate, w_up, w_down)


TILES = dict(
    inproj=dict(tm=1024, tn=256),
    outproj=dict(tm=512),
    ffn=dict(tm=1024, tf=512),
)


def kernel(x_prompt, x_sample, norm1, w_in, q_gain, k_gain, rpb, conv_w, attn_out_gain,
           conv_out_gain, w_out, norm2, w_gate, w_up, w_down):
    depth, d, _ = w_in.shape
    w_in_b, w_out_b, w_gate_b, w_up_b, w_down_b = (
        w.astype(BF16) for w in (w_in, w_out, w_gate, w_up, w_down))
    attn_w = N_HEADS * HEAD_DIM
    w_vt_b = jnp.swapaxes(w_in_b[:, :, 2 * attn_w:3 * attn_w], 1, 2)
    row = lambda a: a.reshape(depth, 1, a.shape[-1])
    norm1_r, norm2_r = row(norm1), row(norm2)
    ag_r, cg_r = row(attn_out_gain), row(conv_out_gain)
    qg_r = row(q_gain.astype(F32) * (HEAD_DIM ** -0.5 * LOG2_E))
    kg_r = row(k_gain.astype(F32))

    streams = []
    for x in (x_prompt, x_sample):
        b, seq, _ = x.shape
        streams.append([x.reshape(b * seq, d), b, seq])

    for layer in range(depth):
        bias_blocks = _bias_blocks(rpb[layer])
        for st in streams:
            x, b, seq = st
            q, k, vt, conv = _in_projection(x, norm1_r, w_in_b, w_vt_b, qg_r, kg_r, conv_w, cg_r,
                                            layer, seq=seq, **TILES["inproj"])
            attn = _attention(q, k, vt, bias_blocks, ag_r, layer, n_rows=seq // GRID_W)
            x = _out_projection(x, attn, conv, w_out_b, layer, **TILES["outproj"])
            st[0] = _feed_forward(x, norm2_r, w_gate_b, w_up_b, w_down_b, layer, **TILES["ffn"])

    return tuple(x.reshape(b, seq, d) for x, b, seq in streams)
```

```python
import functools
import math

import numpy as np
import jax
import jax.numpy as jnp
from jax import lax
from jax.experimental import pallas as pl
from jax.experimental.pallas import tpu as pltpu

GRID_W = 64
N_HEADS = 8
HEAD_DIM = 128
CONV_GROUPS = 8
CONV_K = 3
WIN_ROWS = 8
WIN_COLS = 16
EPS = 1e-6
NEG_INF = -1e30
LOG2_E = math.log2(math.e)

BF16_ROWS = 16
NORM_CHUNK = 512
FFN_NORM_CHUNK = 512
VT_CHUNK = 256
VT_PAD = BF16_ROWS
VMEM_LIMIT = 56 * 1024 * 1024

ROW_GROUP = 4
KEY_ROWS = ROW_GROUP + WIN_ROWS
GROUPS_PER_ITER = 32
N_REL_ROWS = 2 * WIN_ROWS - 1
N_REL_COLS = 2 * WIN_COLS - 1

F32 = jnp.float32
BF16 = jnp.bfloat16


def _params(*sem):
    return pltpu.CompilerParams(dimension_semantics=sem, vmem_limit_bytes=VMEM_LIMIT)


def _rms(x, gain):
    ms = jnp.mean(x * x, axis=-1, keepdims=True)
    return (x * lax.rsqrt(ms + EPS)) * gain


def _inproj_kernel(x_ref, xp_ref, xn_ref, g_ref, wq_ref, wk_ref, wv_ref, wb_ref, wc_ref, wu_ref,
                   qg_ref, kg_ref, cw_ref, cg_ref, q_ref, k_ref, vt_ref, conv_ref, hn_ref,
                   *, tiles_per_seq):
    i = pl.program_id(0)
    j = pl.program_id(1)
    tm = x_ref.shape[0]
    halo = xp_ref.shape[0]

    def head_norm(w_ref, gain_ref, o_ref, hn, r0, nr):
        acc = jnp.dot(hn, w_ref[...], preferred_element_type=F32)
        gain = gain_ref[...]
        for hh in range(acc.shape[-1] // HEAD_DIM):
            a = acc[:, hh * HEAD_DIM:(hh + 1) * HEAD_DIM]
            o_ref[hh, r0:r0 + nr, :] = _rms(a, gain).astype(BF16)

    def v_part(hn):
        acc = lax.dot_general(wv_ref[...], hn, (((1,), (1,)), ((), ())),
                              preferred_element_type=F32)
        ones = jnp.ones((VT_PAD, VT_CHUNK), BF16)
        for hh in range(acc.shape[0] // HEAD_DIM):
            for c in range(acc.shape[1] // VT_CHUNK):
                blk = acc[hh * HEAD_DIM:(hh + 1) * HEAD_DIM, c * VT_CHUNK:(c + 1) * VT_CHUNK]
                vt_ref[hh, c, 0:HEAD_DIM, :] = blk.astype(BF16)
                vt_ref[hh, c, HEAD_DIM:, :] = ones

    def conv_part():
        hn_all = hn_ref[...]
        hn = hn_ref[halo:halo + tm, :]
        cu = (jnp.dot(hn_all, wc_ref[...], preferred_element_type=F32)
              * jnp.dot(hn_all, wu_ref[...], preferred_element_type=F32))
        gate_b = jnp.dot(hn, wb_ref[...], preferred_element_type=F32)
        rows = cu.shape[0]
        before = pltpu.roll(cu, 1, 0)[halo:halo + tm]
        after = pltpu.roll(cu, rows - 1, 0)[halo:halo + tm]
        w = cw_ref[...]
        y = gate_b * (before * w[0:1, :] + cu[halo:halo + tm] * w[1:2, :] + after * w[2:3, :])
        cgain = cg_ref[...]
        for gi in range(y.shape[-1] // HEAD_DIM):
            sl = slice(gi * HEAD_DIM, (gi + 1) * HEAD_DIM)
            conv_ref[:, sl] = _rms(y[:, sl], cgain[:, sl]).astype(BF16)

    @pl.when(j == 0)
    def _():
        g = g_ref[...]
        first = i % tiles_per_seq == 0
        last = i % tiles_per_seq == tiles_per_seq - 1
        hn_ref[0:halo, :] = jnp.where(first, 0.0, _rms(xp_ref[...], g)).astype(BF16)
        hn_ref[halo + tm:, :] = jnp.where(last, 0.0, _rms(xn_ref[...], g)).astype(BF16)
        for r0 in range(0, tm, NORM_CHUNK):
            hn = _rms(x_ref[r0:r0 + NORM_CHUNK, :], g).astype(BF16)
            hn_ref[halo + r0:halo + r0 + NORM_CHUNK, :] = hn
            head_norm(wq_ref, qg_ref, q_ref, hn, r0, NORM_CHUNK)
            head_norm(wk_ref, kg_ref, k_ref, hn, r0, NORM_CHUNK)
        conv_part()
        v_part(hn_ref[halo:halo + tm, :])

    @pl.when(j > 0)
    def _():
        conv_part()
        hn = hn_ref[halo:halo + tm, :]
        head_norm(wq_ref, qg_ref, q_ref, hn, 0, tm)
        head_norm(wk_ref, kg_ref, k_ref, hn, 0, tm)
        v_part(hn)


def _in_projection(x, norm1, w_in, w_vt, q_gain, k_gain, conv_w, conv_gain, layer, *,
                   seq, tm, tn):
    m, d = x.shape
    attn_w = N_HEADS * HEAD_DIM
    conv_width = conv_w.shape[-1]
    assert w_in.shape[-1] == 3 * attn_w + 3 * conv_width and attn_w == conv_width
    assert conv_width // CONV_GROUPS == HEAD_DIM and attn_w % tn == 0 and tn % HEAD_DIM == 0
    assert seq % tm == 0 and m % seq == 0 and tm % NORM_CHUNK == 0
    nj = attn_w // tn
    halo = BF16_ROWS
    per = tm // halo
    n_halo = m // halo
    wspec = lambda part: pl.BlockSpec((None, d, tn), lambda i, j: (layer, 0, part * nj + j))
    gainspec = pl.BlockSpec((None, 1, HEAD_DIM), lambda i, j: (layer, 0, 0))
    headspec = pl.BlockSpec((tn // HEAD_DIM, tm, HEAD_DIM), lambda i, j: (j, i, 0))
    heads = jax.ShapeDtypeStruct((N_HEADS, m, HEAD_DIM), BF16)
    vt_rows = HEAD_DIM + VT_PAD
    vt_shape = jax.ShapeDtypeStruct((N_HEADS, m // VT_CHUNK, vt_rows, VT_CHUNK), BF16)
    vt_spec = pl.BlockSpec((tn // HEAD_DIM, tm // VT_CHUNK, vt_rows, VT_CHUNK),
                           lambda i, j: (j, i, 0, 0))
    return pl.pallas_call(
        functools.partial(_inproj_kernel, tiles_per_seq=seq // tm),
        out_shape=(heads, heads, vt_shape, jax.ShapeDtypeStruct((m, conv_width), BF16)),
        grid=(m // tm, nj),
        in_specs=[
            pl.BlockSpec((tm, d), lambda i, j: (i, 0)),
            pl.BlockSpec((halo, d), lambda i, j: (jnp.maximum(i * per - 1, 0), 0)),
            pl.BlockSpec((halo, d), lambda i, j: (jnp.minimum((i + 1) * per, n_halo - 1), 0)),
            pl.BlockSpec((None, 1, d), lambda i, j: (layer, 0, 0)),
            wspec(0), wspec(1),
            pl.BlockSpec((None, tn, d), lambda i, j: (layer, j, 0)),
            wspec(3), wspec(4), wspec(5),
            gainspec, gainspec,
            pl.BlockSpec((None, CONV_K, tn), lambda i, j: (layer, 0, j)),
            pl.BlockSpec((None, 1, tn), lambda i, j: (layer, 0, j)),
        ],
        out_specs=(headspec, headspec, vt_spec, pl.BlockSpec((tm, tn), lambda i, j: (i, j))),
        scratch_shapes=[pltpu.VMEM((tm + 2 * halo, d), BF16)],
        compiler_params=_params("parallel", "arbitrary"),
        name="in_projection",
    )(x, x, x, norm1, w_in, w_in, w_vt, w_in, w_in, w_in, q_gain, k_gain, conv_w, conv_gain)


def _row_group_variants(n_rows):
    variants = []
    for kind in range(3):
        table = []
        for i in range(ROW_GROUP):
            row = []
            for j in range(KEY_ROWS):
                if kind == 0:
                    q_abs, k_abs = i, j
                elif kind == 1:
                    q_abs, k_abs = KEY_ROWS + i, KEY_ROWS - WIN_ROWS // 2 + j
                else:
                    q_abs, k_abs = n_rows - ROW_GROUP + i, n_rows - KEY_ROWS + j
                r0 = q_abs - WIN_ROWS // 2
                if kind != 1:
                    r0 = min(max(r0, 0), n_rows - WIN_ROWS)
                valid = r0 <= k_abs < r0 + WIN_ROWS
                row.append(k_abs - q_abs + WIN_ROWS - 1 if valid else None)
            table.append(row)
        variants.append(table)
    return variants


def _attn_kernel(q_ref, k_ref, vt_ref, t_ref, g_ref, o_ref, bias_ref, s_ref, acc_ref, *, n_rows):
    n_groups = n_rows // ROW_GROUP
    gq = ROW_GROUP * GRID_W
    gk = KEY_ROWS * GRID_W
    rows_per_chunk = VT_CHUNK // GRID_W

    left = lax.broadcasted_iota(jnp.int32, (GRID_W, 2 * GRID_W), 1) < GRID_W
    neg = jnp.full((GRID_W, 2 * GRID_W), NEG_INF, F32)
    for kind, table in enumerate(_row_group_variants(n_rows)):
        for j in range(KEY_ROWS):
            for ii in range(ROW_GROUP // 2):
                dl, dr = table[2 * ii][j], table[2 * ii + 1][j]
                if dl is None and dr is None:
                    blk = neg
                else:
                    lo = neg if dl is None else t_ref[dl]
                    hi = neg if dr is None else t_ref[dr]
                    blk = jnp.where(left, lo, hi)
                bias_ref[kind, j * GRID_W:(j + 1) * GRID_W,
                         ii * 2 * GRID_W:(ii + 1) * 2 * GRID_W] = blk

    gain = g_ref[...]

    def key_row0(g):
        return jnp.clip(g * ROW_GROUP - WIN_ROWS // 2, 0, n_rows - KEY_ROWS)

    def scores(g):
        kind = jnp.where(g == 0, 0, jnp.where(g == n_groups - 1, 2, 1))
        q = q_ref[pl.ds(pl.multiple_of(g * gq, gq), gq), :]
        k0 = pl.multiple_of(key_row0(g) * GRID_W, VT_CHUNK)
        s = jnp.concatenate([
            lax.dot_general(k_ref[pl.ds(k0 + c * VT_CHUNK, VT_CHUNK), :], q,
                            (((1,), (1,)), ((), ())), preferred_element_type=F32)
            for c in range(gk // VT_CHUNK)], axis=0)
        return s + bias_ref[kind]

    def attend(s, g):
        m = jnp.max(s, axis=0, keepdims=True)
        p = jnp.exp2(s - m).astype(BF16)
        c0 = key_row0(g) // rows_per_chunk
        o = None
        for c in range(gk // VT_CHUNK):
            part = jnp.dot(vt_ref[c0 + c], p[c * VT_CHUNK:(c + 1) * VT_CHUNK, :],
                           preferred_element_type=F32)
            o = part if o is None else o + part
        return o[0:HEAD_DIM] / o[HEAD_DIM:HEAD_DIM + 1]

    def emit(o, g):
        ms = jnp.mean(o * o, axis=0, keepdims=True)
        y = (o * lax.rsqrt(ms + EPS)).T * gain
        o_ref[pl.ds(pl.multiple_of(g * gq, gq), gq), :] = y.astype(o_ref.dtype)

    s_ref[...] = scores(0)
    acc_ref[...] = jnp.zeros_like(acc_ref)

    def body(t, carry):
        g0 = GROUPS_PER_ITER * t
        emit(acc_ref[...], jnp.maximum(g0 - 1, 0))
        s_cur = s_ref[...]
        for u in range(GROUPS_PER_ITER):
            g = g0 + u
            s_next = scores(jnp.minimum(g + 1, n_groups - 1))
            o = attend(s_cur, g)
            if u < GROUPS_PER_ITER - 1:
                emit(o, g)
            else:
                acc_ref[...] = o
            s_cur = s_next
        s_ref[...] = s_cur
        return carry

    lax.fori_loop(0, n_groups // GROUPS_PER_ITER, body, 0)
    emit(acc_ref[...], n_groups - 1)


def _attention(q, k, vt, bias_blocks, out_gain, layer, *, n_rows):
    _, m, _ = q.shape
    seq = n_rows * GRID_W
    b = m // seq
    gq, gk = ROW_GROUP * GRID_W, KEY_ROWS * GRID_W
    assert b * seq == m and n_rows % (GROUPS_PER_ITER * ROW_GROUP) == 0
    assert n_rows >= KEY_ROWS + ROW_GROUP and ROW_GROUP % 2 == 0
    assert seq % VT_CHUNK == 0 and gk % VT_CHUNK == 0 and gq % VT_CHUNK == 0
    assert (WIN_ROWS // 2 * GRID_W) % VT_CHUNK == 0
    blk = pl.BlockSpec((None, seq, HEAD_DIM), lambda h, bi: (h, bi, 0))
    return pl.pallas_call(
        functools.partial(_attn_kernel, n_rows=n_rows),
        out_shape=jax.ShapeDtypeStruct((N_HEADS, m, HEAD_DIM), BF16),
        grid=(N_HEADS, b),
        in_specs=[
            blk, blk,
            pl.BlockSpec((None, seq // VT_CHUNK, HEAD_DIM + VT_PAD, VT_CHUNK), lambda h, bi: (h, bi, 0, 0)),
            pl.BlockSpec((None, N_REL_ROWS, GRID_W, 2 * GRID_W), lambda h, bi: (h, 0, 0, 0)),
            pl.BlockSpec((None, 1, HEAD_DIM), lambda h, bi: (layer, 0, h)),
        ],
        out_specs=blk,
        scratch_shapes=[pltpu.VMEM((3, gk, gq), F32), pltpu.VMEM((gk, gq), F32),
                        pltpu.VMEM((HEAD_DIM, gq), F32)],
        compiler_params=_params("parallel", "parallel"),
        name="neighbourhood_attention",
    )(q, k, vt, bias_blocks, out_gain)


def _bias_blocks(rpb_layer):
    kc = np.arange(GRID_W)[:, None]
    c = np.arange(GRID_W)[None, :]
    start = np.clip(c - WIN_COLS // 2, 0, GRID_W - WIN_COLS)
    valid = (kc >= start) & (kc < start + WIN_COLS)
    select = (np.arange(N_REL_COLS)[:, None, None] == (kc - c + WIN_COLS - 1)[None]) & valid[None]
    t = jnp.einsum("hrd,dkc->hrkc", rpb_layer.astype(F32), jnp.asarray(select, F32),
                   precision=lax.Precision.HIGHEST)
    t = jnp.where(jnp.asarray(valid), t * LOG2_E, NEG_INF)
    return jnp.concatenate([t, t], axis=-1)


def _outproj_kernel(x_ref, a_ref, c_ref, wa_ref, wc_ref, o_ref):
    attn = jnp.concatenate([a_ref[h] for h in range(a_ref.shape[0])], axis=-1)
    acc = jnp.dot(attn, wa_ref[...], preferred_element_type=F32)
    acc = acc + jnp.dot(c_ref[...], wc_ref[...], preferred_element_type=F32)
    o_ref[...] = x_ref[...] + acc


def _out_projection(x, attn, conv, w_out, layer, *, tm):
    m, d = x.shape
    n_heads, _, head_dim = attn.shape
    aw, cw = n_heads * head_dim, conv.shape[-1]
    assert aw == cw and aw + cw == w_out.shape[1] and m % tm == 0
    return pl.pallas_call(
        _outproj_kernel,
        out_shape=jax.ShapeDtypeStruct((m, d), F32),
        grid=(m // tm,),
        in_specs=[
            pl.BlockSpec((tm, d), lambda i: (i, 0)),
            pl.BlockSpec((n_heads, tm, head_dim), lambda i: (0, i, 0)),
            pl.BlockSpec((tm, cw), lambda i: (i, 0)),
            pl.BlockSpec((None, aw, d), lambda i: (layer, 0, 0)),
            pl.BlockSpec((None, cw, d), lambda i: (layer, 1, 0)),
        ],
        out_specs=pl.BlockSpec((tm, d), lambda i: (i, 0)),
        compiler_params=_params("parallel"),
        name="out_projection",
    )(x, attn, conv, w_out, w_out)


def _ffn_kernel(x_ref, g_ref, wg_ref, wu_ref, wd_ref, o_ref, hn_ref):
    j = pl.program_id(1)
    tm = x_ref.shape[0]

    def swiglu(hn):
        gate = jnp.dot(hn, wg_ref[...], preferred_element_type=F32)
        up = jnp.dot(hn, wu_ref[...], preferred_element_type=F32)
        h = (gate * jax.nn.sigmoid(gate)) * up
        return jnp.dot(h.astype(BF16), wd_ref[...], preferred_element_type=F32)

    @pl.when(j == 0)
    def _():
        g = g_ref[...]
        for r0 in range(0, tm, FFN_NORM_CHUNK):
            rows = slice(r0, r0 + FFN_NORM_CHUNK)
            x = x_ref[rows, :]
            hn = _rms(x, g).astype(BF16)
            hn_ref[rows, :] = hn
            o_ref[rows, :] = x + swiglu(hn)

    @pl.when(j > 0)
    def _():
        o_ref[...] += swiglu(hn_ref[...])


def _feed_forward(x, norm2, w_gate, w_up, w_down, layer, *, tm, tf):
    m, d = x.shape
    f = w_gate.shape[-1]
    assert m % tm == 0 and f % tf == 0 and tm % FFN_NORM_CHUNK == 0
    return pl.pallas_call(
        _ffn_kernel,
        out_shape=jax.ShapeDtypeStruct((m, d), F32),
        grid=(m // tm, f // tf),
        in_specs=[
            pl.BlockSpec((tm, d), lambda i, j: (i, 0)),
            pl.BlockSpec((None, 1, d), lambda i, j: (layer, 0, 0)),
            pl.BlockSpec((None, d, tf), lambda i, j: (layer, 0, j)),
            pl.BlockSpec((None, d, tf), lambda i, j: (layer, 0, j)),
            pl.BlockSpec((None, tf, d), lambda i, j: (layer, j, 0)),
        ],
        out_specs=pl.BlockSpec((tm, d), lambda i, j: (i, 0)),
        scratch_shapes=[pltpu.VMEM((tm, d), BF16)],
        compiler_params=_params("parallel", "arbitrary"),
        name="feed_forward",
    )(x, norm2, w_gate, w_up, w_down)


TILES = dict(
    inproj=dict(tm=1024, tn=256),
    outproj=dict(tm=512),
    ffn=dict(tm=1024, tf=512),
)


def kernel(x_prompt, x_sample, norm1, w_in, q_gain, k_gain, rpb, conv_w, attn_out_gain,
           conv_out_gain, w_out, norm2, w_gate, w_up, w_down):
    depth, d, _ = w_in.shape
    w_in_b, w_out_b, w_gate_b, w_up_b, w_down_b = (
        w.astype(BF16) for w in (w_in, w_out, w_gate, w_up, w_down))
    attn_w = N_HEADS * HEAD_DIM
    w_vt_b = jnp.swapaxes(w_in_b[:, :, 2 * attn_w:3 * attn_w], 1, 2)
    row = lambda a: a.reshape(depth, 1, a.shape[-1])
    norm1_r, norm2_r = row(norm1), row(norm2)
    ag_r, cg_r = row(attn_out_gain), row(conv_out_gain)
    qg_r = row(q_gain.astype(F32) * (HEAD_DIM ** -0.5 * LOG2_E))
    kg_r = row(k_gain.astype(F32))

    streams = []
    for x in (x_prompt, x_sample):
        b, seq, _ = x.shape
        streams.append([x.reshape(b * seq, d), b, seq])

    for layer in range(depth):
        bias_blocks = _bias_blocks(rpb[layer])
        for st in streams:
            x, b, seq = st
            q, k, vt, conv = _in_projection(x, norm1_r, w_in_b, w_vt_b, qg_r, kg_r, conv_w, cg_r,
                                            layer, seq=seq, **TILES["inproj"])
            attn = _attention(q, k, vt, bias_blocks, ag_r, layer, n_rows=seq // GRID_W)
            x = _out_projection(x, attn, conv, w_out_b, layer, **TILES["outproj"])
            st[0] = _feed_forward(x, norm2_r, w_gate_b, w_up_b, w_down_b, layer, **TILES["ffn"])

    return tuple(x.reshape(b, seq, d) for x, b, seq in streams)
```

```python
import functools
import math

import numpy as np
import jax
import jax.numpy as jnp
from jax import lax
from jax.experimental import pallas as pl
from jax.experimental.pallas import tpu as pltpu

GRID_W = 64
N_HEADS = 8
HEAD_DIM = 128
CONV_GROUPS = 8
CONV_K = 3
WIN_ROWS = 8
WIN_COLS = 16
EPS = 1e-6
NEG_INF = -1e30
LOG2_E = math.log2(math.e)

BF16_ROWS = 16
NORM_CHUNKS = (512, 512)
VT_CHUNK = 256
VT_PAD = BF16_ROWS
VMEM_LIMIT = 56 * 1024 * 1024

ROW_GROUP = 4
KEY_ROWS = ROW_GROUP + WIN_ROWS
GROUPS_PER_ITER = 32
N_REL_ROWS = 2 * WIN_ROWS - 1
N_REL_COLS = 2 * WIN_COLS - 1

F32 = jnp.float32
BF16 = jnp.bfloat16


def _params(*sem):
    return pltpu.CompilerParams(dimension_semantics=sem, vmem_limit_bytes=VMEM_LIMIT)


def _chunks(tm):
    assert sum(NORM_CHUNKS) == tm and all(n % BF16_ROWS == 0 for n in NORM_CHUNKS)
    starts = np.cumsum((0,) + NORM_CHUNKS[:-1])
    return [(int(r0), nr) for r0, nr in zip(starts, NORM_CHUNKS)]


def _rms(x, gain):
    ms = jnp.mean(x * x, axis=-1, keepdims=True)
    return (x * lax.rsqrt(ms + EPS)) * gain


def _inproj_kernel(x_ref, xp_ref, xn_ref, g_ref, wq_ref, wk_ref, wv_ref, wb_ref, wc_ref, wu_ref,
                   qg_ref, kg_ref, cw_ref, cg_ref, q_ref, k_ref, vt_ref, conv_ref, hn_ref,
                   *, tiles_per_seq):
    i = pl.program_id(0)
    j = pl.program_id(1)
    tm = x_ref.shape[0]
    halo = xp_ref.shape[0]

    def head_norm(w_ref, gain_ref, o_ref, hn, r0, nr):
        acc = jnp.dot(hn, w_ref[...], preferred_element_type=F32)
        gain = gain_ref[...]
        for hh in range(acc.shape[-1] // HEAD_DIM):
            a = acc[:, hh * HEAD_DIM:(hh + 1) * HEAD_DIM]
            o_ref[hh, r0:r0 + nr, :] = _rms(a, gain).astype(BF16)

    def v_part(hn):
        acc = lax.dot_general(wv_ref[...], hn, (((1,), (1,)), ((), ())),
                              preferred_element_type=F32)
        ones = jnp.ones((VT_PAD, VT_CHUNK), BF16)
        for hh in range(acc.shape[0] // HEAD_DIM):
            for c in range(acc.shape[1] // VT_CHUNK):
                blk = acc[hh * HEAD_DIM:(hh + 1) * HEAD_DIM, c * VT_CHUNK:(c + 1) * VT_CHUNK]
                vt_ref[hh, c, 0:HEAD_DIM, :] = blk.astype(BF16)
                vt_ref[hh, c, HEAD_DIM:, :] = ones

    def conv_part():
        hn_all = hn_ref[...]
        hn = hn_ref[halo:halo + tm, :]
        cu = (jnp.dot(hn_all, wc_ref[...], preferred_element_type=F32)
              * jnp.dot(hn_all, wu_ref[...], preferred_element_type=F32))
        gate_b = jnp.dot(hn, wb_ref[...], preferred_element_type=F32)
        rows = cu.shape[0]
        before = pltpu.roll(cu, 1, 0)[halo:halo + tm]
        after = pltpu.roll(cu, rows - 1, 0)[halo:halo + tm]
        w = cw_ref[...]
        y = gate_b * (before * w[0:1, :] + cu[halo:halo + tm] * w[1:2, :] + after * w[2:3, :])
        cgain = cg_ref[...]
        for gi in range(y.shape[-1] // HEAD_DIM):
            sl = slice(gi * HEAD_DIM, (gi + 1) * HEAD_DIM)
            conv_ref[:, sl] = _rms(y[:, sl], cgain[:, sl]).astype(BF16)

    @pl.when(j == 0)
    def _():
        g = g_ref[...]
        first = i % tiles_per_seq == 0
        last = i % tiles_per_seq == tiles_per_seq - 1
        hn_ref[0:halo, :] = jnp.where(first, 0.0, _rms(xp_ref[...], g)).astype(BF16)
        hn_ref[halo + tm:, :] = jnp.where(last, 0.0, _rms(xn_ref[...], g)).astype(BF16)
        for r0, nr in _chunks(tm):
            hn = _rms(x_ref[r0:r0 + nr, :], g).astype(BF16)
            hn_ref[halo + r0:halo + r0 + nr, :] = hn
            head_norm(wq_ref, qg_ref, q_ref, hn, r0, nr)
            head_norm(wk_ref, kg_ref, k_ref, hn, r0, nr)
        conv_part()
        v_part(hn_ref[halo:halo + tm, :])

    @pl.when(j > 0)
    def _():
        conv_part()
        hn = hn_ref[halo:halo + tm, :]
        head_norm(wq_ref, qg_ref, q_ref, hn, 0, tm)
        head_norm(wk_ref, kg_ref, k_ref, hn, 0, tm)
        v_part(hn)


def _in_projection(x, norm1, w_in, w_vt, q_gain, k_gain, conv_w, conv_gain, layer, *,
                   seq, tm, tn):
    m, d = x.shape
    attn_w = N_HEADS * HEAD_DIM
    conv_width = conv_w.shape[-1]
    assert w_in.shape[-1] == 3 * attn_w + 3 * conv_width and attn_w == conv_width
    assert conv_width // CONV_GROUPS == HEAD_DIM and attn_w % tn == 0 and tn % HEAD_DIM == 0
    assert seq % tm == 0 and m % seq == 0
    nj = attn_w // tn
    halo = BF16_ROWS
    per = tm // halo
    n_halo = m // halo
    wspec = lambda part: pl.BlockSpec((None, d, tn), lambda i, j: (layer, 0, part * nj + j))
    gainspec = pl.BlockSpec((None, 1, HEAD_DIM), lambda i, j: (layer, 0, 0))
    headspec = pl.BlockSpec((tn // HEAD_DIM, tm, HEAD_DIM), lambda i, j: (j, i, 0))
    heads = jax.ShapeDtypeStruct((N_HEADS, m, HEAD_DIM), BF16)
    vt_rows = HEAD_DIM + VT_PAD
    vt_shape = jax.ShapeDtypeStruct((N_HEADS, m // VT_CHUNK, vt_rows, VT_CHUNK), BF16)
    vt_spec = pl.BlockSpec((tn // HEAD_DIM, tm // VT_CHUNK, vt_rows, VT_CHUNK),
                           lambda i, j: (j, i, 0, 0))
    return pl.pallas_call(
        functools.partial(_inproj_kernel, tiles_per_seq=seq // tm),
        out_shape=(heads, heads, vt_shape, jax.ShapeDtypeStruct((m, conv_width), BF16)),
        grid=(m // tm, nj),
        in_specs=[
            pl.BlockSpec((tm, d), lambda i, j: (i, 0)),
            pl.BlockSpec((halo, d), lambda i, j: (jnp.maximum(i * per - 1, 0), 0)),
            pl.BlockSpec((halo, d), lambda i, j: (jnp.minimum((i + 1) * per, n_halo - 1), 0)),
            pl.BlockSpec((None, 1, d), lambda i, j: (layer, 0, 0)),
            wspec(0), wspec(1),
            pl.BlockSpec((None, tn, d), lambda i, j: (layer, j, 0)),
            wspec(3), wspec(4), wspec(5),
            gainspec, gainspec,
            pl.BlockSpec((None, CONV_K, tn), lambda i, j: (layer, 0, j)),
            pl.BlockSpec((None, 1, tn), lambda i, j: (layer, 0, j)),
        ],
        out_specs=(headspec, headspec, vt_spec, pl.BlockSpec((tm, tn), lambda i, j: (i, j))),
        scratch_shapes=[pltpu.VMEM((tm + 2 * halo, d), BF16)],
        compiler_params=_params("parallel", "arbitrary"),
        name="in_projection",
    )(x, x, x, norm1, w_in, w_in, w_vt, w_in, w_in, w_in, q_gain, k_gain, conv_w, conv_gain)


def _row_group_variants(n_rows):
    variants = []
    for kind in range(3):
        table = []
        for i in range(ROW_GROUP):
            row = []
            for j in range(KEY_ROWS):
                if kind == 0:
                    q_abs, k_abs = i, j
                elif kind == 1:
                    q_abs, k_abs = KEY_ROWS + i, KEY_ROWS - WIN_ROWS // 2 + j
                else:
                    q_abs, k_abs = n_rows - ROW_GROUP + i, n_rows - KEY_ROWS + j
                r0 = q_abs - WIN_ROWS // 2
                if kind != 1:
                    r0 = min(max(r0, 0), n_rows - WIN_ROWS)
                valid = r0 <= k_abs < r0 + WIN_ROWS
                row.append(k_abs - q_abs + WIN_ROWS - 1 if valid else None)
            table.append(row)
        variants.append(table)
    return variants


def _attn_kernel(q_ref, k_ref, vt_ref, t_ref, g_ref, o_ref, bias_ref, s_ref, acc_ref, *, n_rows):
    n_groups = n_rows // ROW_GROUP
    gq = ROW_GROUP * GRID_W
    gk = KEY_ROWS * GRID_W
    rows_per_chunk = VT_CHUNK // GRID_W

    left = lax.broadcasted_iota(jnp.int32, (GRID_W, 2 * GRID_W), 1) < GRID_W
    neg = jnp.full((GRID_W, 2 * GRID_W), NEG_INF, F32)
    for kind, table in enumerate(_row_group_variants(n_rows)):
        for j in range(KEY_ROWS):
            for ii in range(ROW_GROUP // 2):
                dl, dr = table[2 * ii][j], table[2 * ii + 1][j]
                if dl is None and dr is None:
                    blk = neg
                else:
                    lo = neg if dl is None else t_ref[dl]
                    hi = neg if dr is None else t_ref[dr]
                    blk = jnp.where(left, lo, hi)
                bias_ref[kind, j * GRID_W:(j + 1) * GRID_W,
                         ii * 2 * GRID_W:(ii + 1) * 2 * GRID_W] = blk

    gain = g_ref[...]

    def key_row0(g):
        return jnp.clip(g * ROW_GROUP - WIN_ROWS // 2, 0, n_rows - KEY_ROWS)

    def scores(g):
        kind = jnp.where(g == 0, 0, jnp.where(g == n_groups - 1, 2, 1))
        q = q_ref[pl.ds(pl.multiple_of(g * gq, gq), gq), :]
        k0 = pl.multiple_of(key_row0(g) * GRID_W, VT_CHUNK)
        s = jnp.concatenate([
            lax.dot_general(k_ref[pl.ds(k0 + c * VT_CHUNK, VT_CHUNK), :], q,
                            (((1,), (1,)), ((), ())), preferred_element_type=F32)
            for c in range(gk // VT_CHUNK)], axis=0)
        return s + bias_ref[kind]

    def attend(s, g):
        m = jnp.max(s, axis=0, keepdims=True)
        p = jnp.exp2(s - m).astype(BF16)
        c0 = key_row0(g) // rows_per_chunk
        vt = jnp.concatenate([vt_ref[c0 + c] for c in range(gk // VT_CHUNK)], axis=1)
        o = jnp.dot(vt, p, preferred_element_type=F32)
        return o[0:HEAD_DIM] / o[HEAD_DIM:HEAD_DIM + 1]

    def emit(o, g):
        ms = jnp.mean(o * o, axis=0, keepdims=True)
        y = (o * lax.rsqrt(ms + EPS)).T * gain
        o_ref[pl.ds(pl.multiple_of(g * gq, gq), gq), :] = y.astype(o_ref.dtype)

    s_ref[...] = scores(0)
    acc_ref[...] = jnp.zeros_like(acc_ref)

    def body(t, carry):
        g0 = GROUPS_PER_ITER * t
        emit(acc_ref[...], jnp.maximum(g0 - 1, 0))
        s_cur = s_ref[...]
        for u in range(GROUPS_PER_ITER):
            g = g0 + u
            s_next = scores(jnp.minimum(g + 1, n_groups - 1))
            o = attend(s_cur, g)
            if u < GROUPS_PER_ITER - 1:
                emit(o, g)
            else:
                acc_ref[...] = o
            s_cur = s_next
        s_ref[...] = s_cur
        return carry

    lax.fori_loop(0, n_groups // GROUPS_PER_ITER, body, 0)
    emit(acc_ref[...], n_groups - 1)


def _attention(q, k, vt, bias_blocks, out_gain, layer, *, n_rows):
    _, m, _ = q.shape
    seq = n_rows * GRID_W
    b = m // seq
    gq, gk = ROW_GROUP * GRID_W, KEY_ROWS * GRID_W
    assert b * seq == m and n_rows % (GROUPS_PER_ITER * ROW_GROUP) == 0
    assert n_rows >= KEY_ROWS + ROW_GROUP and ROW_GROUP % 2 == 0
    assert seq % VT_CHUNK == 0 and gk % VT_CHUNK == 0 and gq % VT_CHUNK == 0
    assert (WIN_ROWS // 2 * GRID_W) % VT_CHUNK == 0
    blk = pl.BlockSpec((None, seq, HEAD_DIM), lambda h, bi: (h, bi, 0))
    return pl.pallas_call(
        functools.partial(_attn_kernel, n_rows=n_rows),
        out_shape=jax.ShapeDtypeStruct((N_HEADS, m, HEAD_DIM), BF16),
        grid=(N_HEADS, b),
        in_specs=[
            blk, blk,
            pl.BlockSpec((None, seq // VT_CHUNK, HEAD_DIM + VT_PAD, VT_CHUNK), lambda h, bi: (h, bi, 0, 0)),
            pl.BlockSpec((None, N_REL_ROWS, GRID_W, 2 * GRID_W), lambda h, bi: (h, 0, 0, 0)),
            pl.BlockSpec((None, 1, HEAD_DIM), lambda h, bi: (layer, 0, h)),
        ],
        out_specs=blk,
        scratch_shapes=[pltpu.VMEM((3, gk, gq), F32), pltpu.VMEM((gk, gq), F32),
                        pltpu.VMEM((HEAD_DIM, gq), F32)],
        compiler_params=_params("parallel", "parallel"),
        name="neighbourhood_attention",
    )(q, k, vt, bias_blocks, out_gain)


def _bias_blocks(rpb_layer):
    kc = np.arange(GRID_W)[:, None]
    c = np.arange(GRID_W)[None, :]
    start = np.clip(c - WIN_COLS // 2, 0, GRID_W - WIN_COLS)
    valid = (kc >= start) & (kc < start + WIN_COLS)
    select = (np.arange(N_REL_COLS)[:, None, None] == (kc - c + WIN_COLS - 1)[None]) & valid[None]
    t = jnp.einsum("hrd,dkc->hrkc", rpb_layer.astype(F32), jnp.asarray(select, F32),
                   precision=lax.Precision.HIGHEST)
    t = jnp.where(jnp.asarray(valid), t * LOG2_E, NEG_INF)
    return jnp.concatenate([t, t], axis=-1)


def _outproj_kernel(x_ref, a_ref, c_ref, wa_ref, wc_ref, o_ref):
    attn = jnp.concatenate([a_ref[h] for h in range(a_ref.shape[0])], axis=-1)
    acc = jnp.dot(attn, wa_ref[...], preferred_element_type=F32)
    acc = acc + jnp.dot(c_ref[...], wc_ref[...], preferred_element_type=F32)
    o_ref[...] = x_ref[...] + acc


def _out_projection(x, attn, conv, w_out, layer, *, tm):
    m, d = x.shape
    n_heads, _, head_dim = attn.shape
    aw, cw = n_heads * head_dim, conv.shape[-1]
    assert aw == cw and aw + cw == w_out.shape[1] and m % tm == 0
    return pl.pallas_call(
        _outproj_kernel,
        out_shape=jax.ShapeDtypeStruct((m, d), F32),
        grid=(m // tm,),
        in_specs=[
            pl.BlockSpec((tm, d), lambda i: (i, 0)),
            pl.BlockSpec((n_heads, tm, head_dim), lambda i: (0, i, 0)),
            pl.BlockSpec((tm, cw), lambda i: (i, 0)),
            pl.BlockSpec((None, aw, d), lambda i: (layer, 0, 0)),
            pl.BlockSpec((None, cw, d), lambda i: (layer, 1, 0)),
        ],
        out_specs=pl.BlockSpec((tm, d), lambda i: (i, 0)),
        compiler_params=_params("parallel"),
        name="out_projection",
    )(x, attn, conv, w_out, w_out)


def _ffn_kernel(x_ref, g_ref, wg_ref, wu_ref, wd_ref, o_ref, hn_ref):
    j = pl.program_id(1)
    tm = x_ref.shape[0]

    def swiglu(hn):
        gate = jnp.dot(hn, wg_ref[...], preferred_element_type=F32)
        up = jnp.dot(hn, wu_ref[...], preferred_element_type=F32)
        h = (gate * jax.nn.sigmoid(gate)) * up
        return jnp.dot(h.astype(BF16), wd_ref[...], preferred_element_type=F32)

    @pl.when(j == 0)
    def _():
        g = g_ref[...]
        for r0, nr in _chunks(tm):
            rows = slice(r0, r0 + nr)
            x = x_ref[rows, :]
            hn = _rms(x, g).astype(BF16)
            hn_ref[rows, :] = hn
            o_ref[rows, :] = x + swiglu(hn)

    @pl.when(j > 0)
    def _():
        o_ref[...] += swiglu(hn_ref[...])


def _feed_forward(x, norm2, w_gate, w_up, w_down, layer, *, tm, tf):
    m, d = x.shape
    f = w_gate.shape[-1]
    assert m % tm == 0 and f % tf == 0
    return pl.pallas_call(
        _ffn_kernel,
        out_shape=jax.ShapeDtypeStruct((m, d), F32),
        grid=(m // tm, f // tf),
        in_specs=[
            pl.BlockSpec((tm, d), lambda i, j: (i, 0)),
            pl.BlockSpec((None, 1, d), lambda i, j: (layer, 0, 0)),
            pl.BlockSpec((None, d, tf), lambda i, j: (layer, 0, j)),
            pl.BlockSpec((None, d, tf), lambda i, j: (layer, 0, j)),
            pl.BlockSpec((None, tf, d), lambda i, j: (layer, j, 0)),
        ],
        out_specs=pl.BlockSpec((tm, d), lambda i, j: (i, 0)),
        scratch_shapes=[pltpu.VMEM((tm, d), BF16)],
        compiler_params=_params("parallel", "arbitrary"),
        name="feed_forward",
    )(x, norm2, w_gate, w_up, w_down)


TILES = dict(
    inproj=dict(tm=1024, tn=256),
    outproj=dict(tm=512),
    ffn=dict(tm=1024, tf=512),
)


def kernel(x_prompt, x_sample, norm1, w_in, q_gain, k_gain, rpb, conv_w, attn_out_gain,
           conv_out_gain, w_out, norm2, w_gate, w_up, w_down):
    depth, d, _ = w_in.shape
    w_in_b, w_out_b, w_gate_b, w_up_b, w_down_b = (
        w.astype(BF16) for w in (w_in, w_out, w_gate, w_up, w_down))
    attn_w = N_HEADS * HEAD_DIM
    w_vt_b = jnp.swapaxes(w_in_b[:, :, 2 * attn_w:3 * attn_w], 1, 2)
    row = lambda a: a.reshape(depth, 1, a.shape[-1])
    norm1_r, norm2_r = row(norm1), row(norm2)
    ag_r, cg_r = row(attn_out_gain), row(conv_out_gain)
    qg_r = row(q_gain.astype(F32) * (HEAD_DIM ** -0.5 * LOG2_E))
    kg_r = row(k_gain.astype(F32))

    streams = []
    for x in (x_prompt, x_sample):
        b, seq, _ = x.shape
        streams.append([x.reshape(b * seq, d), b, seq])

    for layer in range(depth):
        bias_blocks = _bias_blocks(rpb[layer])
        for st in streams:
            x, b, seq = st
            q, k, vt, conv = _in_projection(x, norm1_r, w_in_b, w_vt_b, qg_r, kg_r, conv_w, cg_r,
                                            layer, seq=seq, **TILES["inproj"])
            attn = _attention(q, k, vt, bias_blocks, ag_r, layer, n_rows=seq // GRID_W)
            x = _out_projection(x, attn, conv, w_out_b, layer, **TILES["outproj"])
            st[0] = _feed_forward(x, norm2_r, w_gate_b, w_up_b, w_down_b, layer, **TILES["ffn"])

    return tuple(x.reshape(b, seq, d) for x, b, seq in streams)
```

```python
import functools
import math

import numpy as np
import jax
import jax.numpy as jnp
from jax import lax
from jax.experimental import pallas as pl
from jax.experimental.pallas import tpu as pltpu

GRID_W = 64
N_HEADS = 8
HEAD_DIM = 128
CONV_GROUPS = 8
CONV_K = 3
WIN_ROWS = 8
WIN_COLS = 16
EPS = 1e-6
NEG_INF = -1e30
LOG2_E = math.log2(math.e)

BF16_ROWS = 16
NORM_CHUNKS = (512, 512)
VT_CHUNK = 256
VT_PAD = BF16_ROWS
VMEM_LIMIT = 56 * 1024 * 1024

ROW_GROUP = 4
KEY_ROWS = ROW_GROUP + WIN_ROWS
GROUPS_PER_ITER = 32
N_REL_ROWS = 2 * WIN_ROWS - 1
N_REL_COLS = 2 * WIN_COLS - 1

F32 = jnp.float32
BF16 = jnp.bfloat16


def _params(*sem):
    return pltpu.CompilerParams(dimension_semantics=sem, vmem_limit_bytes=VMEM_LIMIT)


def _chunks(tm):
    assert sum(NORM_CHUNKS) == tm and all(n % BF16_ROWS == 0 for n in NORM_CHUNKS)
    starts = np.cumsum((0,) + NORM_CHUNKS[:-1])
    return [(int(r0), nr) for r0, nr in zip(starts, NORM_CHUNKS)]


def _rms(x, gain):
    ms = jnp.mean(x * x, axis=-1, keepdims=True)
    return (x * lax.rsqrt(ms + EPS)) * gain


def _inproj_kernel(x_ref, xp_ref, xn_ref, g_ref, wq_ref, wk_ref, wv_ref, wb_ref, wc_ref, wu_ref,
                   qg_ref, kg_ref, cw_ref, cg_ref, q_ref, k_ref, vt_ref, conv_ref, hn_ref,
                   *, tiles_per_seq):
    i = pl.program_id(0)
    j = pl.program_id(1)
    tm = x_ref.shape[0]
    halo = xp_ref.shape[0]

    def head_norm(w_ref, gain_ref, o_ref, hn, r0, nr):
        acc = jnp.dot(hn, w_ref[...], preferred_element_type=F32)
        gain = gain_ref[...]
        for hh in range(acc.shape[-1] // HEAD_DIM):
            a = acc[:, hh * HEAD_DIM:(hh + 1) * HEAD_DIM]
            o_ref[hh, r0:r0 + nr, :] = _rms(a, gain).astype(BF16)

    def v_part(hn):
        acc = lax.dot_general(wv_ref[...], hn, (((1,), (1,)), ((), ())),
                              preferred_element_type=F32)
        ones = jnp.ones((VT_PAD, VT_CHUNK), BF16)
        for hh in range(acc.shape[0] // HEAD_DIM):
            for c in range(acc.shape[1] // VT_CHUNK):
                blk = acc[hh * HEAD_DIM:(hh + 1) * HEAD_DIM, c * VT_CHUNK:(c + 1) * VT_CHUNK]
                vt_ref[hh, c, 0:HEAD_DIM, :] = blk.astype(BF16)
                vt_ref[hh, c, HEAD_DIM:, :] = ones

    def conv_part():
        hn_all = hn_ref[...]
        hn = hn_ref[halo:halo + tm, :]
        cu = (jnp.dot(hn_all, wc_ref[...], preferred_element_type=F32)
              * jnp.dot(hn_all, wu_ref[...], preferred_element_type=F32))
        gate_b = jnp.dot(hn, wb_ref[...], preferred_element_type=F32)
        rows = cu.shape[0]
        before = pltpu.roll(cu, 1, 0)[halo:halo + tm]
        after = pltpu.roll(cu, rows - 1, 0)[halo:halo + tm]
        w = cw_ref[...]
        y = gate_b * (before * w[0:1, :] + cu[halo:halo + tm] * w[1:2, :] + after * w[2:3, :])
        cgain = cg_ref[...]
        for gi in range(y.shape[-1] // HEAD_DIM):
            sl = slice(gi * HEAD_DIM, (gi + 1) * HEAD_DIM)
            conv_ref[:, sl] = _rms(y[:, sl], cgain[:, sl]).astype(BF16)

    @pl.when(j == 0)
    def _():
        g = g_ref[...]
        first = i % tiles_per_seq == 0
        last = i % tiles_per_seq == tiles_per_seq - 1
        hn_ref[0:halo, :] = jnp.where(first, 0.0, _rms(xp_ref[...], g)).astype(BF16)
        hn_ref[halo + tm:, :] = jnp.where(last, 0.0, _rms(xn_ref[...], g)).astype(BF16)
        for r0, nr in _chunks(tm):
            hn = _rms(x_ref[r0:r0 + nr, :], g).astype(BF16)
            hn_ref[halo + r0:halo + r0 + nr, :] = hn
            head_norm(wq_ref, qg_ref, q_ref, hn, r0, nr)
            head_norm(wk_ref, kg_ref, k_ref, hn, r0, nr)
        conv_part()
        v_part(hn_ref[halo:halo + tm, :])

    @pl.when(j > 0)
    def _():
        conv_part()
        hn = hn_ref[halo:halo + tm, :]
        head_norm(wq_ref, qg_ref, q_ref, hn, 0, tm)
        head_norm(wk_ref, kg_ref, k_ref, hn, 0, tm)
        v_part(hn)


def _in_projection(x, norm1, w_in, w_vt, q_gain, k_gain, conv_w, conv_gain, layer, *,
                   seq, tm, tn):
    m, d = x.shape
    attn_w = N_HEADS * HEAD_DIM
    conv_width = conv_w.shape[-1]
    assert w_in.shape[-1] == 3 * attn_w + 3 * conv_width and attn_w == conv_width
    assert conv_width // CONV_GROUPS == HEAD_DIM and attn_w % tn == 0 and tn % HEAD_DIM == 0
    assert seq % tm == 0 and m % seq == 0
    nj = attn_w // tn
    halo = BF16_ROWS
    per = tm // halo
    n_halo = m // halo
    wspec = lambda part: pl.BlockSpec((None, d, tn), lambda i, j: (layer, 0, part * nj + j))
    gainspec = pl.BlockSpec((None, 1, HEAD_DIM), lambda i, j: (layer, 0, 0))
    headspec = pl.BlockSpec((tn // HEAD_DIM, tm, HEAD_DIM), lambda i, j: (j, i, 0))
    heads = jax.ShapeDtypeStruct((N_HEADS, m, HEAD_DIM), BF16)
    vt_rows = HEAD_DIM + VT_PAD
    vt_shape = jax.ShapeDtypeStruct((N_HEADS, m // VT_CHUNK, vt_rows, VT_CHUNK), BF16)
    vt_spec = pl.BlockSpec((tn // HEAD_DIM, tm // VT_CHUNK, vt_rows, VT_CHUNK),
                           lambda i, j: (j, i, 0, 0))
    return pl.pallas_call(
        functools.partial(_inproj_kernel, tiles_per_seq=seq // tm),
        out_shape=(heads, heads, vt_shape, jax.ShapeDtypeStruct((m, conv_width), BF16)),
        grid=(m // tm, nj),
        in_specs=[
            pl.BlockSpec((tm, d), lambda i, j: (i, 0)),
            pl.BlockSpec((halo, d), lambda i, j: (jnp.maximum(i * per - 1, 0), 0)),
            pl.BlockSpec((halo, d), lambda i, j: (jnp.minimum((i + 1) * per, n_halo - 1), 0)),
            pl.BlockSpec((None, 1, d), lambda i, j: (layer, 0, 0)),
            wspec(0), wspec(1),
            pl.BlockSpec((None, tn, d), lambda i, j: (layer, j, 0)),
            wspec(3), wspec(4), wspec(5),
            gainspec, gainspec,
            pl.BlockSpec((None, CONV_K, tn), lambda i, j: (layer, 0, j)),
            pl.BlockSpec((None, 1, tn), lambda i, j: (layer, 0, j)),
        ],
        out_specs=(headspec, headspec, vt_spec, pl.BlockSpec((tm, tn), lambda i, j: (i, j))),
        scratch_shapes=[pltpu.VMEM((tm + 2 * halo, d), BF16)],
        compiler_params=_params("parallel", "arbitrary"),
        name="in_projection",
    )(x, x, x, norm1, w_in, w_in, w_vt, w_in, w_in, w_in, q_gain, k_gain, conv_w, conv_gain)


def _row_group_variants(n_rows):
    variants = []
    for kind in range(3):
        table = []
        for i in range(ROW_GROUP):
            row = []
            for j in range(KEY_ROWS):
                if kind == 0:
                    q_abs, k_abs = i, j
                elif kind == 1:
                    q_abs, k_abs = KEY_ROWS + i, KEY_ROWS - WIN_ROWS // 2 + j
                else:
                    q_abs, k_abs = n_rows - ROW_GROUP + i, n_rows - KEY_ROWS + j
                r0 = q_abs - WIN_ROWS // 2
                if kind != 1:
                    r0 = min(max(r0, 0), n_rows - WIN_ROWS)
                valid = r0 <= k_abs < r0 + WIN_ROWS
                row.append(k_abs - q_abs + WIN_ROWS - 1 if valid else None)
            table.append(row)
        variants.append(table)
    return variants


def _attn_kernel(q_ref, k_ref, vt_ref, t_ref, g_ref, o_ref, bias_ref, s_ref, acc_ref, *, n_rows):
    n_groups = n_rows // ROW_GROUP
    gq = ROW_GROUP * GRID_W
    gk = KEY_ROWS * GRID_W
    rows_per_chunk = VT_CHUNK // GRID_W

    left = lax.broadcasted_iota(jnp.int32, (GRID_W, 2 * GRID_W), 1) < GRID_W
    neg = jnp.full((GRID_W, 2 * GRID_W), NEG_INF, F32)
    for kind, table in enumerate(_row_group_variants(n_rows)):
        for j in range(KEY_ROWS):
            for ii in range(ROW_GROUP // 2):
                dl, dr = table[2 * ii][j], table[2 * ii + 1][j]
                if dl is None and dr is None:
                    blk = neg
                else:
                    lo = neg if dl is None else t_ref[dl]
                    hi = neg if dr is None else t_ref[dr]
                    blk = jnp.where(left, lo, hi)
                bias_ref[kind, j * GRID_W:(j + 1) * GRID_W,
                         ii * 2 * GRID_W:(ii + 1) * 2 * GRID_W] = blk

    gain = g_ref[...]
    variants = _row_group_variants(n_rows)
    windows = []
    for ii in range(ROW_GROUP // 2):
        seen = [j for j in range(KEY_ROWS)
                if variants[1][2 * ii][j] is not None or variants[1][2 * ii + 1][j] is not None]
        windows.append((min(seen), max(seen) + 1))

    def key_row0(g):
        return jnp.clip(g * ROW_GROUP - WIN_ROWS // 2, 0, n_rows - KEY_ROWS)

    def logits(g):
        q = q_ref[pl.ds(pl.multiple_of(g * gq, gq), gq), :]
        k0 = pl.multiple_of(key_row0(g) * GRID_W, VT_CHUNK)
        return jnp.concatenate([
            lax.dot_general(k_ref[pl.ds(k0 + c * VT_CHUNK, VT_CHUNK), :], q,
                            (((1,), (1,)), ((), ())), preferred_element_type=F32)
            for c in range(gk // VT_CHUNK)], axis=0)

    def weights_interior(s):
        cols = []
        for ii, (lo, hi) in enumerate(windows):
            lanes = slice(ii * 2 * GRID_W, (ii + 1) * 2 * GRID_W)
            rows = slice(lo * GRID_W, hi * GRID_W)
            sc = s[rows, lanes] + bias_ref[1, rows, lanes]
            m = jnp.max(sc, axis=0, keepdims=True)
            pieces = [jnp.exp2(sc - m).astype(BF16)]
            if lo:
                pieces.insert(0, jnp.zeros((lo * GRID_W, 2 * GRID_W), BF16))
            if hi < KEY_ROWS:
                pieces.append(jnp.zeros(((KEY_ROWS - hi) * GRID_W, 2 * GRID_W), BF16))
            cols.append(jnp.concatenate(pieces, axis=0))
        return jnp.concatenate(cols, axis=1)

    def weights_edge(s, kind):
        s = s + bias_ref[kind]
        m = jnp.max(s, axis=0, keepdims=True)
        return jnp.exp2(s - m).astype(BF16)

    def attend(p, g):
        c0 = key_row0(g) // rows_per_chunk
        vt = jnp.concatenate([vt_ref[c0 + c] for c in range(gk // VT_CHUNK)], axis=1)
        o = jnp.dot(vt, p, preferred_element_type=F32)
        return o[0:HEAD_DIM] / o[HEAD_DIM:HEAD_DIM + 1]

    def emit(o, g):
        ms = jnp.mean(o * o, axis=0, keepdims=True)
        y = (o * lax.rsqrt(ms + EPS)).T * gain
        o_ref[pl.ds(pl.multiple_of(g * gq, gq), gq), :] = y.astype(o_ref.dtype)

    s_ref[...] = logits(0)
    acc_ref[...] = jnp.zeros_like(acc_ref)

    def body(t, carry):
        g0 = GROUPS_PER_ITER * t
        emit(acc_ref[...], jnp.maximum(g0 - 1, 0))
        s_cur = s_ref[...]
        for u in range(GROUPS_PER_ITER):
            g = g0 + u
            s_next = logits(jnp.minimum(g + 1, n_groups - 1))
            o = attend(weights_interior(s_cur), g)
            if u < GROUPS_PER_ITER - 1:
                emit(o, g)
            else:
                acc_ref[...] = o
            s_cur = s_next
        s_ref[...] = s_cur
        return carry

    lax.fori_loop(0, n_groups // GROUPS_PER_ITER, body, 0)

    for g, kind in ((0, 0), (n_groups - 1, 2)):
        g = jnp.int32(g)
        emit(attend(weights_edge(logits(g), kind), g), g)


def _attention(q, k, vt, bias_blocks, out_gain, layer, *, n_rows):
    _, m, _ = q.shape
    seq = n_rows * GRID_W
    b = m // seq
    gq, gk = ROW_GROUP * GRID_W, KEY_ROWS * GRID_W
    assert b * seq == m and n_rows % (GROUPS_PER_ITER * ROW_GROUP) == 0
    assert n_rows >= KEY_ROWS + ROW_GROUP and ROW_GROUP % 2 == 0
    assert seq % VT_CHUNK == 0 and gk % VT_CHUNK == 0 and gq % VT_CHUNK == 0
    assert (WIN_ROWS // 2 * GRID_W) % VT_CHUNK == 0
    blk = pl.BlockSpec((None, seq, HEAD_DIM), lambda h, bi: (h, bi, 0))
    return pl.pallas_call(
        functools.partial(_attn_kernel, n_rows=n_rows),
        out_shape=jax.ShapeDtypeStruct((N_HEADS, m, HEAD_DIM), BF16),
        grid=(N_HEADS, b),
        in_specs=[
            blk, blk,
            pl.BlockSpec((None, seq // VT_CHUNK, HEAD_DIM + VT_PAD, VT_CHUNK), lambda h, bi: (h, bi, 0, 0)),
            pl.BlockSpec((None, N_REL_ROWS, GRID_W, 2 * GRID_W), lambda h, bi: (h, 0, 0, 0)),
            pl.BlockSpec((None, 1, HEAD_DIM), lambda h, bi: (layer, 0, h)),
        ],
        out_specs=blk,
        scratch_shapes=[pltpu.VMEM((3, gk, gq), F32), pltpu.VMEM((gk, gq), F32),
                        pltpu.VMEM((HEAD_DIM, gq), F32)],
        compiler_params=_params("parallel", "parallel"),
        name="neighbourhood_attention",
    )(q, k, vt, bias_blocks, out_gain)


def _bias_blocks(rpb_layer):
    kc = np.arange(GRID_W)[:, None]
    c = np.arange(GRID_W)[None, :]
    start = np.clip(c - WIN_COLS // 2, 0, GRID_W - WIN_COLS)
    valid = (kc >= start) & (kc < start + WIN_COLS)
    select = (np.arange(N_REL_COLS)[:, None, None] == (kc - c + WIN_COLS - 1)[None]) & valid[None]
    t = jnp.einsum("hrd,dkc->hrkc", rpb_layer.astype(F32), jnp.asarray(select, F32),
                   precision=lax.Precision.HIGHEST)
    t = jnp.where(jnp.asarray(valid), t * LOG2_E, NEG_INF)
    return jnp.concatenate([t, t], axis=-1)


def _outproj_kernel(x_ref, a_ref, c_ref, wa_ref, wc_ref, o_ref):
    attn = jnp.concatenate([a_ref[h] for h in range(a_ref.shape[0])], axis=-1)
    acc = jnp.dot(attn, wa_ref[...], preferred_element_type=F32)
    acc = acc + jnp.dot(c_ref[...], wc_ref[...], preferred_element_type=F32)
    o_ref[...] = x_ref[...] + acc


def _out_projection(x, attn, conv, w_out, layer, *, tm):
    m, d = x.shape
    n_heads, _, head_dim = attn.shape
    aw, cw = n_heads * head_dim, conv.shape[-1]
    assert aw == cw and aw + cw == w_out.shape[1] and m % tm == 0
    return pl.pallas_call(
        _outproj_kernel,
        out_shape=jax.ShapeDtypeStruct((m, d), F32),
        grid=(m // tm,),
        in_specs=[
            pl.BlockSpec((tm, d), lambda i: (i, 0)),
            pl.BlockSpec((n_heads, tm, head_dim), lambda i: (0, i, 0)),
            pl.BlockSpec((tm, cw), lambda i: (i, 0)),
            pl.BlockSpec((None, aw, d), lambda i: (layer, 0, 0)),
            pl.BlockSpec((None, cw, d), lambda i: (layer, 1, 0)),
        ],
        out_specs=pl.BlockSpec((tm, d), lambda i: (i, 0)),
        compiler_params=_params("parallel"),
        name="out_projection",
    )(x, attn, conv, w_out, w_out)


def _ffn_kernel(x_ref, g_ref, wg_ref, wu_ref, wd_ref, o_ref, hn_ref):
    j = pl.program_id(1)
    tm = x_ref.shape[0]

    def swiglu(hn):
        gate = jnp.dot(hn, wg_ref[...], preferred_element_type=F32)
        up = jnp.dot(hn, wu_ref[...], preferred_element_type=F32)
        h = (gate * jax.nn.sigmoid(gate)) * up
        return jnp.dot(h.astype(BF16), wd_ref[...], preferred_element_type=F32)

    @pl.when(j == 0)
    def _():
        g = g_ref[...]
        for r0, nr in _chunks(tm):
            rows = slice(r0, r0 + nr)
            x = x_ref[rows, :]
            hn = _rms(x, g).astype(BF16)
            hn_ref[rows, :] = hn
            o_ref[rows, :] = x + swiglu(hn)

    @pl.when(j > 0)
    def _():
        o_ref[...] += swiglu(hn_ref[...])


def _feed_forward(x, norm2, w_gate, w_up, w_down, layer, *, tm, tf):
    m, d = x.shape
    f = w_gate.shape[-1]
    assert m % tm == 0 and f % tf == 0
    return pl.pallas_call(
        _ffn_kernel,
        out_shape=jax.ShapeDtypeStruct((m, d), F32),
        grid=(m // tm, f // tf),
        in_specs=[
            pl.BlockSpec((tm, d), lambda i, j: (i, 0)),
            pl.BlockSpec((None, 1, d), lambda i, j: (layer, 0, 0)),
            pl.BlockSpec((None, d, tf), lambda i, j: (layer, 0, j)),
            pl.BlockSpec((None, d, tf), lambda i, j: (layer, 0, j)),
            pl.BlockSpec((None, tf, d), lambda i, j: (layer, j, 0)),
        ],
        out_specs=pl.BlockSpec((tm, d), lambda i, j: (i, 0)),
        scratch_shapes=[pltpu.VMEM((tm, d), BF16)],
        compiler_params=_params("parallel", "arbitrary"),
        name="feed_forward",
    )(x, norm2, w_gate, w_up, w_down)


TILES = dict(
    inproj=dict(tm=1024, tn=256),
    outproj=dict(tm=512),
    ffn=dict(tm=1024, tf=512),
)


def kernel(x_prompt, x_sample, norm1, w_in, q_gain, k_gain, rpb, conv_w, attn_out_gain,
           conv_out_gain, w_out, norm2, w_gate, w_up, w_down):
    depth, d, _ = w_in.shape
    w_in_b, w_out_b, w_gate_b, w_up_b, w_down_b = (
        w.astype(BF16) for w in (w_in, w_out, w_gate, w_up, w_down))
    attn_w = N_HEADS * HEAD_DIM
    w_vt_b = jnp.swapaxes(w_in_b[:, :, 2 * attn_w:3 * attn_w], 1, 2)
    row = lambda a: a.reshape(depth, 1, a.shape[-1])
    norm1_r, norm2_r = row(norm1), row(norm2)
    ag_r, cg_r = row(attn_out_gain), row(conv_out_gain)
    qg_r = row(q_gain.astype(F32) * (HEAD_DIM ** -0.5 * LOG2_E))
    kg_r = row(k_gain.astype(F32))

    streams = []
    for x in (x_prompt, x_sample):
        b, seq, _ = x.shape
        streams.append([x.reshape(b * seq, d), b, seq])

    for layer in range(depth):
        bias_blocks = _bias_blocks(rpb[layer])
        for st in streams:
            x, b, seq = st
            q, k, vt, conv = _in_projection(x, norm1_r, w_in_b, w_vt_b, qg_r, kg_r, conv_w, cg_r,
                                            layer, seq=seq, **TILES["inproj"])
            attn = _attention(q, k, vt, bias_blocks, ag_r, layer, n_rows=seq // GRID_W)
            x = _out_projection(x, attn, conv, w_out_b, layer, **TILES["outproj"])
            st[0] = _feed_forward(x, norm2_r, w_gate_b, w_up_b, w_down_b, layer, **TILES["ffn"])

    return tuple(x.reshape(b, seq, d) for x, b, seq in streams)
```

```python
import functools
import math

import numpy as np
import jax
import jax.numpy as jnp
from jax import lax
from jax.experimental import pallas as pl
from jax.experimental.pallas import tpu as pltpu

GRID_W = 64
N_HEADS = 8
HEAD_DIM = 128
CONV_GROUPS = 8
CONV_K = 3
WIN_ROWS = 8
WIN_COLS = 16
EPS = 1e-6
NEG_INF = -1e30
LOG2_E = math.log2(math.e)

BF16_ROWS = 16
NORM_CHUNKS = (512, 512)
VT_CHUNK = 256
VT_PAD = BF16_ROWS
VMEM_LIMIT = 56 * 1024 * 1024

ROW_GROUP = 4
KEY_ROWS = ROW_GROUP + WIN_ROWS
GROUPS_PER_ITER = 32
N_REL_ROWS = 2 * WIN_ROWS - 1
N_REL_COLS = 2 * WIN_COLS - 1

F32 = jnp.float32
BF16 = jnp.bfloat16


def _params(*sem):
    return pltpu.CompilerParams(dimension_semantics=sem, vmem_limit_bytes=VMEM_LIMIT)


def _chunks(tm):
    assert sum(NORM_CHUNKS) == tm and all(n % BF16_ROWS == 0 for n in NORM_CHUNKS)
    starts = np.cumsum((0,) + NORM_CHUNKS[:-1])
    return [(int(r0), nr) for r0, nr in zip(starts, NORM_CHUNKS)]


def _rms(x, gain):
    ms = jnp.mean(x * x, axis=-1, keepdims=True)
    return (x * lax.rsqrt(ms + EPS)) * gain


def _inproj_kernel(x_ref, xp_ref, xn_ref, g_ref, wq_ref, wk_ref, wv_ref, wb_ref, wc_ref, wu_ref,
                   qg_ref, kg_ref, cw_ref, cg_ref, q_ref, k_ref, vt_ref, conv_ref, hn_ref,
                   *, tiles_per_seq):
    i = pl.program_id(0)
    j = pl.program_id(1)
    tm = x_ref.shape[0]
    halo = xp_ref.shape[0]

    def head_norm(w_ref, gain_ref, o_ref, hn, r0, nr):
        acc = jnp.dot(hn, w_ref[...], preferred_element_type=F32)
        gain = gain_ref[...]
        for hh in range(acc.shape[-1] // HEAD_DIM):
            a = acc[:, hh * HEAD_DIM:(hh + 1) * HEAD_DIM]
            o_ref[hh, r0:r0 + nr, :] = _rms(a, gain).astype(BF16)

    def v_part(hn):
        acc = lax.dot_general(wv_ref[...], hn, (((1,), (1,)), ((), ())),
                              preferred_element_type=F32)
        ones = jnp.ones((VT_PAD, VT_CHUNK), BF16)
        for hh in range(acc.shape[0] // HEAD_DIM):
            for c in range(acc.shape[1] // VT_CHUNK):
                blk = acc[hh * HEAD_DIM:(hh + 1) * HEAD_DIM, c * VT_CHUNK:(c + 1) * VT_CHUNK]
                vt_ref[hh, c, 0:HEAD_DIM, :] = blk.astype(BF16)
                vt_ref[hh, c, HEAD_DIM:, :] = ones

    def conv_part():
        hn_all = hn_ref[...]
        hn = hn_ref[halo:halo + tm, :]
        cu = (jnp.dot(hn_all, wc_ref[...], preferred_element_type=F32)
              * jnp.dot(hn_all, wu_ref[...], preferred_element_type=F32))
        gate_b = jnp.dot(hn, wb_ref[...], preferred_element_type=F32)
        rows = cu.shape[0]
        before = pltpu.roll(cu, 1, 0)[halo:halo + tm]
        after = pltpu.roll(cu, rows - 1, 0)[halo:halo + tm]
        w = cw_ref[...]
        y = gate_b * (before * w[0:1, :] + cu[halo:halo + tm] * w[1:2, :] + after * w[2:3, :])
        cgain = cg_ref[...]
        for gi in range(y.shape[-1] // HEAD_DIM):
            sl = slice(gi * HEAD_DIM, (gi + 1) * HEAD_DIM)
            conv_ref[:, sl] = _rms(y[:, sl], cgain[:, sl]).astype(BF16)

    @pl.when(j == 0)
    def _():
        g = g_ref[...]
        first = i % tiles_per_seq == 0
        last = i % tiles_per_seq == tiles_per_seq - 1
        hn_ref[0:halo, :] = jnp.where(first, 0.0, _rms(xp_ref[...], g)).astype(BF16)
        hn_ref[halo + tm:, :] = jnp.where(last, 0.0, _rms(xn_ref[...], g)).astype(BF16)
        for r0, nr in _chunks(tm):
            hn = _rms(x_ref[r0:r0 + nr, :], g).astype(BF16)
            hn_ref[halo + r0:halo + r0 + nr, :] = hn
            head_norm(wq_ref, qg_ref, q_ref, hn, r0, nr)
            head_norm(wk_ref, kg_ref, k_ref, hn, r0, nr)
        conv_part()
        v_part(hn_ref[halo:halo + tm, :])

    @pl.when(j > 0)
    def _():
        conv_part()
        hn = hn_ref[halo:halo + tm, :]
        head_norm(wq_ref, qg_ref, q_ref, hn, 0, tm)
        head_norm(wk_ref, kg_ref, k_ref, hn, 0, tm)
        v_part(hn)


def _in_projection(x, norm1, w_in, w_vt, q_gain, k_gain, conv_w, conv_gain, layer, *,
                   seq, tm, tn):
    m, d = x.shape
    attn_w = N_HEADS * HEAD_DIM
    conv_width = conv_w.shape[-1]
    assert w_in.shape[-1] == 3 * attn_w + 3 * conv_width and attn_w == conv_width
    assert conv_width // CONV_GROUPS == HEAD_DIM and attn_w % tn == 0 and tn % HEAD_DIM == 0
    assert seq % tm == 0 and m % seq == 0
    nj = attn_w // tn
    halo = BF16_ROWS
    per = tm // halo
    n_halo = m // halo
    wspec = lambda part: pl.BlockSpec((None, d, tn), lambda i, j: (layer, 0, part * nj + j))
    gainspec = pl.BlockSpec((None, 1, HEAD_DIM), lambda i, j: (layer, 0, 0))
    headspec = pl.BlockSpec((tn // HEAD_DIM, tm, HEAD_DIM), lambda i, j: (j, i, 0))
    heads = jax.ShapeDtypeStruct((N_HEADS, m, HEAD_DIM), BF16)
    vt_rows = HEAD_DIM + VT_PAD
    vt_shape = jax.ShapeDtypeStruct((N_HEADS, m // VT_CHUNK, vt_rows, VT_CHUNK), BF16)
    vt_spec = pl.BlockSpec((tn // HEAD_DIM, tm // VT_CHUNK, vt_rows, VT_CHUNK),
                           lambda i, j: (j, i, 0, 0))
    return pl.pallas_call(
        functools.partial(_inproj_kernel, tiles_per_seq=seq // tm),
        out_shape=(heads, heads, vt_shape, jax.ShapeDtypeStruct((m, conv_width), BF16)),
        grid=(m // tm, nj),
        in_specs=[
            pl.BlockSpec((tm, d), lambda i, j: (i, 0)),
            pl.BlockSpec((halo, d), lambda i, j: (jnp.maximum(i * per - 1, 0), 0)),
            pl.BlockSpec((halo, d), lambda i, j: (jnp.minimum((i + 1) * per, n_halo - 1), 0)),
            pl.BlockSpec((None, 1, d), lambda i, j: (layer, 0, 0)),
            wspec(0), wspec(1),
            pl.BlockSpec((None, tn, d), lambda i, j: (layer, j, 0)),
            wspec(3), wspec(4), wspec(5),
            gainspec, gainspec,
            pl.BlockSpec((None, CONV_K, tn), lambda i, j: (layer, 0, j)),
            pl.BlockSpec((None, 1, tn), lambda i, j: (layer, 0, j)),
        ],
        out_specs=(headspec, headspec, vt_spec, pl.BlockSpec((tm, tn), lambda i, j: (i, j))),
        scratch_shapes=[pltpu.VMEM((tm + 2 * halo, d), BF16)],
        compiler_params=_params("parallel", "arbitrary"),
        name="in_projection",
    )(x, x, x, norm1, w_in, w_in, w_vt, w_in, w_in, w_in, q_gain, k_gain, conv_w, conv_gain)


def _row_group_variants(n_rows):
    variants = []
    for kind in range(3):
        table = []
        for i in range(ROW_GROUP):
            row = []
            for j in range(KEY_ROWS):
                if kind == 0:
                    q_abs, k_abs = i, j
                elif kind == 1:
                    q_abs, k_abs = KEY_ROWS + i, KEY_ROWS - WIN_ROWS // 2 + j
                else:
                    q_abs, k_abs = n_rows - ROW_GROUP + i, n_rows - KEY_ROWS + j
                r0 = q_abs - WIN_ROWS // 2
                if kind != 1:
                    r0 = min(max(r0, 0), n_rows - WIN_ROWS)
                valid = r0 <= k_abs < r0 + WIN_ROWS
                row.append(k_abs - q_abs + WIN_ROWS - 1 if valid else None)
            table.append(row)
        variants.append(table)
    return variants


def _interior_windows(n_rows):
    table = _row_group_variants(n_rows)[1]
    windows = []
    for ii in range(ROW_GROUP // 2):
        seen = [j for j in range(KEY_ROWS)
                if table[2 * ii][j] is not None or table[2 * ii + 1][j] is not None]
        windows.append((min(seen), max(seen) + 1))
    return windows


def _attn_kernel(q_ref, k_ref, vt_ref, t_ref, g_ref, o_ref, bias_ref, s_ref, acc_ref, *, n_rows):
    n_groups = n_rows // ROW_GROUP
    gq = ROW_GROUP * GRID_W
    gk = KEY_ROWS * GRID_W
    rows_per_chunk = VT_CHUNK // GRID_W

    left = lax.broadcasted_iota(jnp.int32, (GRID_W, 2 * GRID_W), 1) < GRID_W
    neg = jnp.full((GRID_W, 2 * GRID_W), NEG_INF, F32)
    for kind, table in enumerate(_row_group_variants(n_rows)):
        for j in range(KEY_ROWS):
            for ii in range(ROW_GROUP // 2):
                dl, dr = table[2 * ii][j], table[2 * ii + 1][j]
                if dl is None and dr is None:
                    blk = neg
                else:
                    lo = neg if dl is None else t_ref[dl]
                    hi = neg if dr is None else t_ref[dr]
                    blk = jnp.where(left, lo, hi)
                bias_ref[kind, j * GRID_W:(j + 1) * GRID_W,
                         ii * 2 * GRID_W:(ii + 1) * 2 * GRID_W] = blk

    gain = g_ref[...]
    windows = _interior_windows(n_rows)
    interior_keys = s_ref.shape[0]

    def key_row0(g):
        return jnp.clip(g * ROW_GROUP - WIN_ROWS // 2, 0, n_rows - KEY_ROWS)

    def logits(g, n_keys):
        q = q_ref[pl.ds(pl.multiple_of(g * gq, gq), gq), :]
        k0 = pl.multiple_of(key_row0(g) * GRID_W, VT_CHUNK)
        return jnp.concatenate([
            lax.dot_general(k_ref[pl.ds(k0 + c, min(VT_CHUNK, n_keys - c)), :], q,
                            (((1,), (1,)), ((), ())), preferred_element_type=F32)
            for c in range(0, n_keys, VT_CHUNK)], axis=0)

    def weights_interior(s):
        cols = []
        for ii, (lo, hi) in enumerate(windows):
            lanes = slice(ii * 2 * GRID_W, (ii + 1) * 2 * GRID_W)
            rows = slice(lo * GRID_W, hi * GRID_W)
            sc = s[rows, lanes] + bias_ref[1, rows, lanes]
            m = jnp.max(sc, axis=0, keepdims=True)
            pieces = [jnp.exp2(sc - m).astype(BF16)]
            if lo:
                pieces.insert(0, jnp.zeros((lo * GRID_W, 2 * GRID_W), BF16))
            if hi < KEY_ROWS:
                pieces.append(jnp.zeros(((KEY_ROWS - hi) * GRID_W, 2 * GRID_W), BF16))
            cols.append(jnp.concatenate(pieces, axis=0))
        return jnp.concatenate(cols, axis=1)

    def weights_edge(s, kind):
        s = s + bias_ref[kind]
        m = jnp.max(s, axis=0, keepdims=True)
        return jnp.exp2(s - m).astype(BF16)

    def attend(p, g):
        c0 = key_row0(g) // rows_per_chunk
        vt = jnp.concatenate([vt_ref[c0 + c] for c in range(gk // VT_CHUNK)], axis=1)
        o = jnp.dot(vt, p, preferred_element_type=F32)
        return o[0:HEAD_DIM] / o[HEAD_DIM:HEAD_DIM + 1]

    def emit(o, g):
        ms = jnp.mean(o * o, axis=0, keepdims=True)
        y = (o * lax.rsqrt(ms + EPS)).T * gain
        o_ref[pl.ds(pl.multiple_of(g * gq, gq), gq), :] = y.astype(o_ref.dtype)

    s_ref[...] = logits(0, interior_keys)
    acc_ref[...] = jnp.zeros_like(acc_ref)

    def body(t, carry):
        g0 = GROUPS_PER_ITER * t
        emit(acc_ref[...], jnp.maximum(g0 - 1, 0))
        s_cur = s_ref[...]
        for u in range(GROUPS_PER_ITER):
            g = g0 + u
            s_next = logits(jnp.minimum(g + 1, n_groups - 1), interior_keys)
            o = attend(weights_interior(s_cur), g)
            if u < GROUPS_PER_ITER - 1:
                emit(o, g)
            else:
                acc_ref[...] = o
            s_cur = s_next
        s_ref[...] = s_cur
        return carry

    lax.fori_loop(0, n_groups // GROUPS_PER_ITER, body, 0)

    for g, kind in ((0, 0), (n_groups - 1, 2)):
        g = jnp.int32(g)
        emit(attend(weights_edge(logits(g, gk), kind), g), g)


def _attention(q, k, vt, bias_blocks, out_gain, layer, *, n_rows):
    _, m, _ = q.shape
    seq = n_rows * GRID_W
    b = m // seq
    gq, gk = ROW_GROUP * GRID_W, KEY_ROWS * GRID_W
    assert b * seq == m and n_rows % (GROUPS_PER_ITER * ROW_GROUP) == 0
    assert n_rows >= KEY_ROWS + ROW_GROUP and ROW_GROUP % 2 == 0
    assert seq % VT_CHUNK == 0 and gk % VT_CHUNK == 0 and gq % VT_CHUNK == 0
    assert (WIN_ROWS // 2 * GRID_W) % VT_CHUNK == 0
    interior_keys = max(hi for _, hi in _interior_windows(n_rows)) * GRID_W
    blk = pl.BlockSpec((None, seq, HEAD_DIM), lambda h, bi: (h, bi, 0))
    return pl.pallas_call(
        functools.partial(_attn_kernel, n_rows=n_rows),
        out_shape=jax.ShapeDtypeStruct((N_HEADS, m, HEAD_DIM), BF16),
        grid=(N_HEADS, b),
        in_specs=[
            blk, blk,
            pl.BlockSpec((None, seq // VT_CHUNK, HEAD_DIM + VT_PAD, VT_CHUNK), lambda h, bi: (h, bi, 0, 0)),
            pl.BlockSpec((None, N_REL_ROWS, GRID_W, 2 * GRID_W), lambda h, bi: (h, 0, 0, 0)),
            pl.BlockSpec((None, 1, HEAD_DIM), lambda h, bi: (layer, 0, h)),
        ],
        out_specs=blk,
        scratch_shapes=[pltpu.VMEM((3, gk, gq), F32), pltpu.VMEM((interior_keys, gq), F32),
                        pltpu.VMEM((HEAD_DIM, gq), F32)],
        compiler_params=_params("parallel", "parallel"),
        name="neighbourhood_attention",
    )(q, k, vt, bias_blocks, out_gain)


def _bias_blocks(rpb_layer):
    kc = np.arange(GRID_W)[:, None]
    c = np.arange(GRID_W)[None, :]
    start = np.clip(c - WIN_COLS // 2, 0, GRID_W - WIN_COLS)
    valid = (kc >= start) & (kc < start + WIN_COLS)
    select = (np.arange(N_REL_COLS)[:, None, None] == (kc - c + WIN_COLS - 1)[None]) & valid[None]
    t = jnp.einsum("hrd,dkc->hrkc", rpb_layer.astype(F32), jnp.asarray(select, F32),
                   precision=lax.Precision.HIGHEST)
    t = jnp.where(jnp.asarray(valid), t * LOG2_E, NEG_INF)
    return jnp.concatenate([t, t], axis=-1)


def _outproj_kernel(x_ref, a_ref, c_ref, wa_ref, wc_ref, o_ref):
    attn = jnp.concatenate([a_ref[h] for h in range(a_ref.shape[0])], axis=-1)
    acc = jnp.dot(attn, wa_ref[...], preferred_element_type=F32)
    acc = acc + jnp.dot(c_ref[...], wc_ref[...], preferred_element_type=F32)
    o_ref[...] = x_ref[...] + acc


def _out_projection(x, attn, conv, w_out, layer, *, tm):
    m, d = x.shape
    n_heads, _, head_dim = attn.shape
    aw, cw = n_heads * head_dim, conv.shape[-1]
    assert aw == cw and aw + cw == w_out.shape[1] and m % tm == 0
    return pl.pallas_call(
        _outproj_kernel,
        out_shape=jax.ShapeDtypeStruct((m, d), F32),
        grid=(m // tm,),
        in_specs=[
            pl.BlockSpec((tm, d), lambda i: (i, 0)),
            pl.BlockSpec((n_heads, tm, head_dim), lambda i: (0, i, 0)),
            pl.BlockSpec((tm, cw), lambda i: (i, 0)),
            pl.BlockSpec((None, aw, d), lambda i: (layer, 0, 0)),
            pl.BlockSpec((None, cw, d), lambda i: (layer, 1, 0)),
        ],
        out_specs=pl.BlockSpec((tm, d), lambda i: (i, 0)),
        compiler_params=_params("parallel"),
        name="out_projection",
    )(x, attn, conv, w_out, w_out)


def _ffn_kernel(x_ref, g_ref, wg_ref, wu_ref, wd_ref, o_ref, hn_ref):
    j = pl.program_id(1)
    tm = x_ref.shape[0]

    def swiglu(hn):
        gate = jnp.dot(hn, wg_ref[...], preferred_element_type=F32)
        up = jnp.dot(hn, wu_ref[...], preferred_element_type=F32)
        h = (gate * jax.nn.sigmoid(gate)) * up
        return jnp.dot(h.astype(BF16), wd_ref[...], preferred_element_type=F32)

    @pl.when(j == 0)
    def _():
        g = g_ref[...]
        for r0, nr in _chunks(tm):
            rows = slice(r0, r0 + nr)
            x = x_ref[rows, :]
            hn = _rms(x, g).astype(BF16)
            hn_ref[rows, :] = hn
            o_ref[rows, :] = x + swiglu(hn)

    @pl.when(j > 0)
    def _():
        o_ref[...] += swiglu(hn_ref[...])


def _feed_forward(x, norm2, w_gate, w_up, w_down, layer, *, tm, tf):
    m, d = x.shape
    f = w_gate.shape[-1]
    assert m % tm == 0 and f % tf == 0
    return pl.pallas_call(
        _ffn_kernel,
        out_shape=jax.ShapeDtypeStruct((m, d), F32),
        grid=(m // tm, f // tf),
        in_specs=[
            pl.BlockSpec((tm, d), lambda i, j: (i, 0)),
            pl.BlockSpec((None, 1, d), lambda i, j: (layer, 0, 0)),
            pl.BlockSpec((None, d, tf), lambda i, j: (layer, 0, j)),
            pl.BlockSpec((None, d, tf), lambda i, j: (layer, 0, j)),
            pl.BlockSpec((None, tf, d), lambda i, j: (layer, j, 0)),
        ],
        out_specs=pl.BlockSpec((tm, d), lambda i, j: (i, 0)),
        scratch_shapes=[pltpu.VMEM((tm, d), BF16)],
        compiler_params=_params("parallel", "arbitrary"),
        name="feed_forward",
    )(x, norm2, w_gate, w_up, w_down)


TILES = dict(
    inproj=dict(tm=1024, tn=256),
    outproj=dict(tm=512),
    ffn=dict(tm=1024, tf=512),
)


def kernel(x_prompt, x_sample, norm1, w_in, q_gain, k_gain, rpb, conv_w, attn_out_gain,
           conv_out_gain, w_out, norm2, w_gate, w_up, w_down):
    depth, d, _ = w_in.shape
    w_in_b, w_out_b, w_gate_b, w_up_b, w_down_b = (
        w.astype(BF16) for w in (w_in, w_out, w_gate, w_up, w_down))
    attn_w = N_HEADS * HEAD_DIM
    w_vt_b = jnp.swapaxes(w_in_b[:, :, 2 * attn_w:3 * attn_w], 1, 2)
    row = lambda a: a.reshape(depth, 1, a.shape[-1])
    norm1_r, norm2_r = row(norm1), row(norm2)
    ag_r, cg_r = row(attn_out_gain), row(conv_out_gain)
    qg_r = row(q_gain.astype(F32) * (HEAD_DIM ** -0.5 * LOG2_E))
    kg_r = row(k_gain.astype(F32))

    streams = []
    for x in (x_prompt, x_sample):
        b, seq, _ = x.shape
        streams.append([x.reshape(b * seq, d), b, seq])

    for layer in range(depth):
        bias_blocks = _bias_blocks(rpb[layer])
        for st in streams:
            x, b, seq = st
            q, k, vt, conv = _in_projection(x, norm1_r, w_in_b, w_vt_b, qg_r, kg_r, conv_w, cg_r,
                                            layer, seq=seq, **TILES["inproj"])
            attn = _attention(q, k, vt, bias_blocks, ag_r, layer, n_rows=seq // GRID_W)
            x = _out_projection(x, attn, conv, w_out_b, layer, **TILES["outproj"])
            st[0] = _feed_forward(x, norm2_r, w_gate_b, w_up_b, w_down_b, layer, **TILES["ffn"])

    return tuple(x.reshape(b, seq, d) for x, b, seq in streams)
```

```python
import functools
import math

import numpy as np
import jax
import jax.numpy as jnp
from jax import lax
from jax.experimental import pallas as pl
from jax.experimental.pallas import tpu as pltpu

GRID_W = 64
N_HEADS = 8
HEAD_DIM = 128
CONV_GROUPS = 8
CONV_K = 3
WIN_ROWS = 8
WIN_COLS = 16
EPS = 1e-6
NEG_INF = -1e30
LOG2_E = math.log2(math.e)

BF16_ROWS = 16
NORM_CHUNKS = (512, 512)
VT_CHUNK = 256
VT_PAD = BF16_ROWS
VMEM_LIMIT = 56 * 1024 * 1024

ROW_GROUP = 4
KEY_ROWS = ROW_GROUP + WIN_ROWS
GROUPS_PER_ITER = 32
N_REL_ROWS = 2 * WIN_ROWS - 1
N_REL_COLS = 2 * WIN_COLS - 1

F32 = jnp.float32
BF16 = jnp.bfloat16


def _params(*sem):
    return pltpu.CompilerParams(dimension_semantics=sem, vmem_limit_bytes=VMEM_LIMIT)


def _chunks(tm):
    assert sum(NORM_CHUNKS) == tm and all(n % BF16_ROWS == 0 for n in NORM_CHUNKS)
    starts = np.cumsum((0,) + NORM_CHUNKS[:-1])
    return [(int(r0), nr) for r0, nr in zip(starts, NORM_CHUNKS)]


def _rms(x, gain):
    ms = jnp.mean(x * x, axis=-1, keepdims=True)
    return (x * lax.rsqrt(ms + EPS)) * gain


def _inproj_kernel(x_ref, xp_ref, xn_ref, g_ref, wq_ref, wk_ref, wv_ref, wb_ref, wc_ref, wu_ref,
                   qg_ref, kg_ref, cw_ref, cg_ref, q_ref, k_ref, vt_ref, conv_ref, hn_ref,
                   *, tiles_per_seq):
    i = pl.program_id(0)
    j = pl.program_id(1)
    tm = x_ref.shape[0]
    halo = xp_ref.shape[0]

    def head_norm(w_ref, gain_ref, o_ref, hn, r0, nr):
        acc = jnp.dot(hn, w_ref[...], preferred_element_type=F32)
        gain = gain_ref[...]
        for hh in range(acc.shape[-1] // HEAD_DIM):
            a = acc[:, hh * HEAD_DIM:(hh + 1) * HEAD_DIM]
            o_ref[hh, r0:r0 + nr, :] = _rms(a, gain).astype(BF16)

    def v_part(hn):
        acc = lax.dot_general(wv_ref[...], hn, (((1,), (1,)), ((), ())),
                              preferred_element_type=F32)
        ones = jnp.ones((VT_PAD, VT_CHUNK), BF16)
        for hh in range(acc.shape[0] // HEAD_DIM):
            for c in range(acc.shape[1] // VT_CHUNK):
                blk = acc[hh * HEAD_DIM:(hh + 1) * HEAD_DIM, c * VT_CHUNK:(c + 1) * VT_CHUNK]
                vt_ref[hh, c, 0:HEAD_DIM, :] = blk.astype(BF16)
                vt_ref[hh, c, HEAD_DIM:, :] = ones

    def conv_part():
        hn_all = hn_ref[...]
        hn = hn_ref[halo:halo + tm, :]
        cu = (jnp.dot(hn_all, wc_ref[...], preferred_element_type=F32)
              * jnp.dot(hn_all, wu_ref[...], preferred_element_type=F32))
        gate_b = jnp.dot(hn, wb_ref[...], preferred_element_type=F32)
        rows = cu.shape[0]
        before = pltpu.roll(cu, 1, 0)[halo:halo + tm]
        after = pltpu.roll(cu, rows - 1, 0)[halo:halo + tm]
        w = cw_ref[...]
        y = gate_b * (before * w[0:1, :] + cu[halo:halo + tm] * w[1:2, :] + after * w[2:3, :])
        cgain = cg_ref[...]
        for gi in range(y.shape[-1] // HEAD_DIM):
            sl = slice(gi * HEAD_DIM, (gi + 1) * HEAD_DIM)
            conv_ref[:, sl] = _rms(y[:, sl], cgain[:, sl]).astype(BF16)

    @pl.when(j == 0)
    def _():
        g = g_ref[...]
        first = i % tiles_per_seq == 0
        last = i % tiles_per_seq == tiles_per_seq - 1
        hn_ref[0:halo, :] = jnp.where(first, 0.0, _rms(xp_ref[...], g)).astype(BF16)
        hn_ref[halo + tm:, :] = jnp.where(last, 0.0, _rms(xn_ref[...], g)).astype(BF16)
        for r0, nr in _chunks(tm):
            hn = _rms(x_ref[r0:r0 + nr, :], g).astype(BF16)
            hn_ref[halo + r0:halo + r0 + nr, :] = hn
            head_norm(wq_ref, qg_ref, q_ref, hn, r0, nr)
            head_norm(wk_ref, kg_ref, k_ref, hn, r0, nr)
        conv_part()
        v_part(hn_ref[halo:halo + tm, :])

    @pl.when(j > 0)
    def _():
        conv_part()
        hn = hn_ref[halo:halo + tm, :]
        head_norm(wq_ref, qg_ref, q_ref, hn, 0, tm)
        head_norm(wk_ref, kg_ref, k_ref, hn, 0, tm)
        v_part(hn)


def _in_projection(x, norm1, w_in, w_vt, q_gain, k_gain, conv_w, conv_gain, layer, *,
                   seq, tm, tn):
    m, d = x.shape
    attn_w = N_HEADS * HEAD_DIM
    conv_width = conv_w.shape[-1]
    assert w_in.shape[-1] == 3 * attn_w + 3 * conv_width and attn_w == conv_width
    assert conv_width // CONV_GROUPS == HEAD_DIM and attn_w % tn == 0 and tn % HEAD_DIM == 0
    assert seq % tm == 0 and m % seq == 0
    nj = attn_w // tn
    halo = BF16_ROWS
    per = tm // halo
    n_halo = m // halo
    wspec = lambda part: pl.BlockSpec((None, d, tn), lambda i, j: (layer, 0, part * nj + j))
    gainspec = pl.BlockSpec((None, 1, HEAD_DIM), lambda i, j: (layer, 0, 0))
    headspec = pl.BlockSpec((tn // HEAD_DIM, tm, HEAD_DIM), lambda i, j: (j, i, 0))
    heads = jax.ShapeDtypeStruct((N_HEADS, m, HEAD_DIM), BF16)
    vt_rows = HEAD_DIM + VT_PAD
    vt_shape = jax.ShapeDtypeStruct((N_HEADS, m // VT_CHUNK, vt_rows, VT_CHUNK), BF16)
    vt_spec = pl.BlockSpec((tn // HEAD_DIM, tm // VT_CHUNK, vt_rows, VT_CHUNK),
                           lambda i, j: (j, i, 0, 0))
    return pl.pallas_call(
        functools.partial(_inproj_kernel, tiles_per_seq=seq // tm),
        out_shape=(heads, heads, vt_shape, jax.ShapeDtypeStruct((m, conv_width), BF16)),
        grid=(m // tm, nj),
        in_specs=[
            pl.BlockSpec((tm, d), lambda i, j: (i, 0)),
            pl.BlockSpec((halo, d), lambda i, j: (jnp.maximum(i * per - 1, 0), 0)),
            pl.BlockSpec((halo, d), lambda i, j: (jnp.minimum((i + 1) * per, n_halo - 1), 0)),
            pl.BlockSpec((None, 1, d), lambda i, j: (layer, 0, 0)),
            wspec(0), wspec(1),
            pl.BlockSpec((None, tn, d), lambda i, j: (layer, j, 0)),
            wspec(3), wspec(4), wspec(5),
            gainspec, gainspec,
            pl.BlockSpec((None, CONV_K, tn), lambda i, j: (layer, 0, j)),
            pl.BlockSpec((None, 1, tn), lambda i, j: (layer, 0, j)),
        ],
        out_specs=(headspec, headspec, vt_spec, pl.BlockSpec((tm, tn), lambda i, j: (i, j))),
        scratch_shapes=[pltpu.VMEM((tm + 2 * halo, d), BF16)],
        compiler_params=_params("parallel", "arbitrary"),
        name="in_projection",
    )(x, x, x, norm1, w_in, w_in, w_vt, w_in, w_in, w_in, q_gain, k_gain, conv_w, conv_gain)


def _row_group_variants(n_rows):
    variants = []
    for kind in range(3):
        table = []
        for i in range(ROW_GROUP):
            row = []
            for j in range(KEY_ROWS):
                if kind == 0:
                    q_abs, k_abs = i, j
                elif kind == 1:
                    q_abs, k_abs = KEY_ROWS + i, KEY_ROWS - WIN_ROWS // 2 + j
                else:
                    q_abs, k_abs = n_rows - ROW_GROUP + i, n_rows - KEY_ROWS + j
                r0 = q_abs - WIN_ROWS // 2
                if kind != 1:
                    r0 = min(max(r0, 0), n_rows - WIN_ROWS)
                valid = r0 <= k_abs < r0 + WIN_ROWS
                row.append(k_abs - q_abs + WIN_ROWS - 1 if valid else None)
            table.append(row)
        variants.append(table)
    return variants


def _attn_kernel(q_ref, k_ref, vt_ref, t_ref, g_ref, o_ref, bias_ref, s_ref, acc_ref, *, n_rows):
    n_groups = n_rows // ROW_GROUP
    gq = ROW_GROUP * GRID_W
    gk = KEY_ROWS * GRID_W
    rows_per_chunk = VT_CHUNK // GRID_W

    left = lax.broadcasted_iota(jnp.int32, (GRID_W, 2 * GRID_W), 1) < GRID_W
    neg = jnp.full((GRID_W, 2 * GRID_W), NEG_INF, F32)
    for kind, table in enumerate(_row_group_variants(n_rows)):
        for j in range(KEY_ROWS):
            for ii in range(ROW_GROUP // 2):
                dl, dr = table[2 * ii][j], table[2 * ii + 1][j]
                if dl is None and dr is None:
                    blk = neg
                else:
                    lo = neg if dl is None else t_ref[dl]
                    hi = neg if dr is None else t_ref[dr]
                    blk = jnp.where(left, lo, hi)
                bias_ref[kind, j * GRID_W:(j + 1) * GRID_W,
                         ii * 2 * GRID_W:(ii + 1) * 2 * GRID_W] = blk

    gain = g_ref[...]
    variants = _row_group_variants(n_rows)
    windows = []
    for ii in range(ROW_GROUP // 2):
        seen = [j for j in range(KEY_ROWS)
                if variants[1][2 * ii][j] is not None or variants[1][2 * ii + 1][j] is not None]
        windows.append((min(seen), max(seen) + 1))

    def key_row0(g):
        return jnp.clip(g * ROW_GROUP - WIN_ROWS // 2, 0, n_rows - KEY_ROWS)

    def logits(g):
        q = q_ref[pl.ds(pl.multiple_of(g * gq, gq), gq), :]
        k0 = pl.multiple_of(key_row0(g) * GRID_W, VT_CHUNK)
        return jnp.concatenate([
            lax.dot_general(k_ref[pl.ds(k0 + c * VT_CHUNK, VT_CHUNK), :], q,
                            (((1,), (1,)), ((), ())), preferred_element_type=F32)
            for c in range(gk // VT_CHUNK)], axis=0)

    def weights_interior(s):
        cols = []
        for ii, (lo, hi) in enumerate(windows):
            lanes = slice(ii * 2 * GRID_W, (ii + 1) * 2 * GRID_W)
            rows = slice(lo * GRID_W, hi * GRID_W)
            sc = s[rows, lanes] + bias_ref[1, rows, lanes]
            m = jnp.max(sc, axis=0, keepdims=True)
            pieces = [jnp.exp2(sc - m).astype(BF16)]
            if lo:
                pieces.insert(0, jnp.zeros((lo * GRID_W, 2 * GRID_W), BF16))
            if hi < KEY_ROWS:
                pieces.append(jnp.zeros(((KEY_ROWS - hi) * GRID_W, 2 * GRID_W), BF16))
            cols.append(jnp.concatenate(pieces, axis=0))
        return jnp.concatenate(cols, axis=1)

    def weights_edge(s, kind):
        s = s + bias_ref[kind]
        m = jnp.max(s, axis=0, keepdims=True)
        return jnp.exp2(s - m).astype(BF16)

    def attend(p, g):
        c0 = key_row0(g) // rows_per_chunk
        vt = jnp.concatenate([vt_ref[c0 + c] for c in range(gk // VT_CHUNK)], axis=1)
        o = jnp.dot(vt, p, preferred_element_type=F32)
        return o[0:HEAD_DIM] / o[HEAD_DIM:HEAD_DIM + 1]

    def emit(o, g):
        ms = jnp.mean(o * o, axis=0, keepdims=True)
        y = (o * lax.rsqrt(ms + EPS)).T * gain
        o_ref[pl.ds(pl.multiple_of(g * gq, gq), gq), :] = y.astype(o_ref.dtype)

    s_ref[...] = logits(0)
    acc_ref[...] = jnp.zeros_like(acc_ref)

    def body(t, carry):
        g0 = GROUPS_PER_ITER * t
        emit(acc_ref[...], jnp.maximum(g0 - 1, 0))
        s_cur = s_ref[...]
        for u in range(GROUPS_PER_ITER):
            g = g0 + u
            s_next = logits(jnp.minimum(g + 1, n_groups - 1))
            o = attend(weights_interior(s_cur), g)
            if u < GROUPS_PER_ITER - 1:
                emit(o, g)
            else:
                acc_ref[...] = o
            s_cur = s_next
        s_ref[...] = s_cur
        return carry

    lax.fori_loop(0, n_groups // GROUPS_PER_ITER, body, 0)

    for g, kind in ((0, 0), (n_groups - 1, 2)):
        g = jnp.int32(g)
        emit(attend(weights_edge(logits(g), kind), g), g)


def _attention(q, k, vt, bias_blocks, out_gain, layer, *, n_rows):
    _, m, _ = q.shape
    seq = n_rows * GRID_W
    b = m // seq
    gq, gk = ROW_GROUP * GRID_W, KEY_ROWS * GRID_W
    assert b * seq == m and n_rows % (GROUPS_PER_ITER * ROW_GROUP) == 0
    assert n_rows >= KEY_ROWS + ROW_GROUP and ROW_GROUP % 2 == 0
    assert seq % VT_CHUNK == 0 and gk % VT_CHUNK == 0 and gq % VT_CHUNK == 0
    assert (WIN_ROWS // 2 * GRID_W) % VT_CHUNK == 0
    blk = pl.BlockSpec((None, seq, HEAD_DIM), lambda h, bi: (h, bi, 0))
    return pl.pallas_call(
        functools.partial(_attn_kernel, n_rows=n_rows),
        out_shape=jax.ShapeDtypeStruct((N_HEADS, m, HEAD_DIM), BF16),
        grid=(N_HEADS, b),
        in_specs=[
            blk, blk,
            pl.BlockSpec((None, seq // VT_CHUNK, HEAD_DIM + VT_PAD, VT_CHUNK), lambda h, bi: (h, bi, 0, 0)),
            pl.BlockSpec((None, N_REL_ROWS, GRID_W, 2 * GRID_W), lambda h, bi: (h, 0, 0, 0)),
            pl.BlockSpec((None, 1, HEAD_DIM), lambda h, bi: (layer, 0, h)),
        ],
        out_specs=blk,
        scratch_shapes=[pltpu.VMEM((3, gk, gq), F32), pltpu.VMEM((gk, gq), F32),
                        pltpu.VMEM((HEAD_DIM, gq), F32)],
        compiler_params=_params("parallel", "parallel"),
        name="neighbourhood_attention",
    )(q, k, vt, bias_blocks, out_gain)


def _bias_blocks(rpb_layer):
    kc = np.arange(GRID_W)[:, None]
    c = np.arange(GRID_W)[None, :]
    start = np.clip(c - WIN_COLS // 2, 0, GRID_W - WIN_COLS)
    valid = (kc >= start) & (kc < start + WIN_COLS)
    select = (np.arange(N_REL_COLS)[:, None, None] == (kc - c + WIN_COLS - 1)[None]) & valid[None]
    t = jnp.einsum("hrd,dkc->hrkc", rpb_layer.astype(F32), jnp.asarray(select, F32),
                   precision=lax.Precision.HIGHEST)
    t = jnp.where(jnp.asarray(valid), t * LOG2_E, NEG_INF)
    return jnp.concatenate([t, t], axis=-1)


def _outproj_kernel(x_ref, a_ref, c_ref, w_ref, o_ref):
    lhs = jnp.concatenate([a_ref[h] for h in range(a_ref.shape[0])] + [c_ref[...]], axis=-1)
    o_ref[...] = x_ref[...] + jnp.dot(lhs, w_ref[...], preferred_element_type=F32)


def _out_projection(x, attn, conv, w_out, layer, *, tm):
    m, d = x.shape
    n_heads, _, head_dim = attn.shape
    aw, cw = n_heads * head_dim, conv.shape[-1]
    assert aw == cw and aw + cw == w_out.shape[1] and m % tm == 0
    return pl.pallas_call(
        _outproj_kernel,
        out_shape=jax.ShapeDtypeStruct((m, d), F32),
        grid=(m // tm,),
        in_specs=[
            pl.BlockSpec((tm, d), lambda i: (i, 0)),
            pl.BlockSpec((n_heads, tm, head_dim), lambda i: (0, i, 0)),
            pl.BlockSpec((tm, cw), lambda i: (i, 0)),
            pl.BlockSpec((None, aw + cw, d), lambda i: (layer, 0, 0)),
        ],
        out_specs=pl.BlockSpec((tm, d), lambda i: (i, 0)),
        compiler_params=_params("parallel"),
        name="out_projection",
    )(x, attn, conv, w_out)


def _ffn_kernel(x_ref, g_ref, wg_ref, wu_ref, wd_ref, o_ref, hn_ref):
    j = pl.program_id(1)
    tm = x_ref.shape[0]

    def swiglu(hn):
        gate = jnp.dot(hn, wg_ref[...], preferred_element_type=F32)
        up = jnp.dot(hn, wu_ref[...], preferred_element_type=F32)
        h = (gate * jax.nn.sigmoid(gate)) * up
        return jnp.dot(h.astype(BF16), wd_ref[...], preferred_element_type=F32)

    @pl.when(j == 0)
    def _():
        g = g_ref[...]
        for r0, nr in _chunks(tm):
            rows = slice(r0, r0 + nr)
            x = x_ref[rows, :]
            hn = _rms(x, g).astype(BF16)
            hn_ref[rows, :] = hn
            o_ref[rows, :] = x + swiglu(hn)

    @pl.when(j > 0)
    def _():
        o_ref[...] += swiglu(hn_ref[...])


def _feed_forward(x, norm2, w_gate, w_up, w_down, layer, *, tm, tf):
    m, d = x.shape
    f = w_gate.shape[-1]
    assert m % tm == 0 and f % tf == 0
    return pl.pallas_call(
        _ffn_kernel,
        out_shape=jax.ShapeDtypeStruct((m, d), F32),
        grid=(m // tm, f // tf),
        in_specs=[
            pl.BlockSpec((tm, d), lambda i, j: (i, 0)),
            pl.BlockSpec((None, 1, d), lambda i, j: (layer, 0, 0)),
            pl.BlockSpec((None, d, tf), lambda i, j: (layer, 0, j)),
            pl.BlockSpec((None, d, tf), lambda i, j: (layer, 0, j)),
            pl.BlockSpec((None, tf, d), lambda i, j: (layer, j, 0)),
        ],
        out_specs=pl.BlockSpec((tm, d), lambda i, j: (i, 0)),
        scratch_shapes=[pltpu.VMEM((tm, d), BF16)],
        compiler_params=_params("parallel", "arbitrary"),
        name="feed_forward",
    )(x, norm2, w_gate, w_up, w_down)


TILES = dict(
    inproj=dict(tm=1024, tn=256),
    outproj=dict(tm=1024),
    ffn=dict(tm=1024, tf=512),
)


def kernel(x_prompt, x_sample, norm1, w_in, q_gain, k_gain, rpb, conv_w, attn_out_gain,
           conv_out_gain, w_out, norm2, w_gate, w_up, w_down):
    depth, d, _ = w_in.shape
    w_in_b, w_out_b, w_gate_b, w_up_b, w_down_b = (
        w.astype(BF16) for w in (w_in, w_out, w_gate, w_up, w_down))
    attn_w = N_HEADS * HEAD_DIM
    w_vt_b = jnp.swapaxes(w_in_b[:, :, 2 * attn_w:3 * attn_w], 1, 2)
    row = lambda a: a.reshape(depth, 1, a.shape[-1])
    norm1_r, norm2_r = row(norm1), row(norm2)
    ag_r, cg_r = row(attn_out_gain), row(conv_out_gain)
    qg_r = row(q_gain.astype(F32) * (HEAD_DIM ** -0.5 * LOG2_E))
    kg_r = row(k_gain.astype(F32))

    streams = []
    for x in (x_prompt, x_sample):
        b, seq, _ = x.shape
        streams.append([x.reshape(b * seq, d), b, seq])

    for layer in range(depth):
        bias_blocks = _bias_blocks(rpb[layer])
        for st in streams:
            x, b, seq = st
            q, k, vt, conv = _in_projection(x, norm1_r, w_in_b, w_vt_b, qg_r, kg_r, conv_w, cg_r,
                                            layer, seq=seq, **TILES["inproj"])
            attn = _attention(q, k, vt, bias_blocks, ag_r, layer, n_rows=seq // GRID_W)
            x = _out_projection(x, attn, conv, w_out_b, layer, **TILES["outproj"])
            st[0] = _feed_forward(x, norm2_r, w_gate_b, w_up_b, w_down_b, layer, **TILES["ffn"])

    return tuple(x.reshape(b, seq, d) for x, b, seq in streams)
```

```python
import functools
import math

import numpy as np
import jax
import jax.numpy as jnp
from jax import lax
from jax.experimental import pallas as pl
from jax.experimental.pallas import tpu as pltpu

GRID_W = 64
N_HEADS = 8
HEAD_DIM = 128
CONV_GROUPS = 8
CONV_K = 3
WIN_ROWS = 8
WIN_COLS = 16
EPS = 1e-6
NEG_INF = -1e30
LOG2_E = math.log2(math.e)

BF16_ROWS = 16
NORM_CHUNKS = (512, 512)
VT_CHUNK = 256
VT_PAD = BF16_ROWS
VMEM_LIMIT = 60 * 1024 * 1024

ROW_GROUP = 4
KEY_ROWS = ROW_GROUP + WIN_ROWS
GROUPS_PER_ITER = 32
N_REL_ROWS = 2 * WIN_ROWS - 1
N_REL_COLS = 2 * WIN_COLS - 1

F32 = jnp.float32
BF16 = jnp.bfloat16


def _params(*sem):
    return pltpu.CompilerParams(dimension_semantics=sem, vmem_limit_bytes=VMEM_LIMIT)


def _chunks(tm):
    assert sum(NORM_CHUNKS) == tm and all(n % BF16_ROWS == 0 for n in NORM_CHUNKS)
    starts = np.cumsum((0,) + NORM_CHUNKS[:-1])
    return [(int(r0), nr) for r0, nr in zip(starts, NORM_CHUNKS)]


def _rms(x, gain):
    ms = jnp.mean(x * x, axis=-1, keepdims=True)
    return (x * lax.rsqrt(ms + EPS)) * gain


def _inproj_kernel(x_ref, xp_ref, xn_ref, g_ref, wq_ref, wk_ref, wv_ref, wb_ref, wc_ref, wu_ref,
                   qg_ref, kg_ref, cw_ref, cg_ref, q_ref, k_ref, vt_ref, conv_ref, hn_ref,
                   *, tiles_per_seq):
    i = pl.program_id(0)
    j = pl.program_id(1)
    tm = x_ref.shape[0]
    halo = xp_ref.shape[0]

    def head_norm(w_ref, gain_ref, o_ref, hn, r0, nr):
        acc = jnp.dot(hn, w_ref[...], preferred_element_type=F32)
        gain = gain_ref[...]
        for hh in range(acc.shape[-1] // HEAD_DIM):
            a = acc[:, hh * HEAD_DIM:(hh + 1) * HEAD_DIM]
            o_ref[hh, r0:r0 + nr, :] = _rms(a, gain).astype(BF16)

    def v_part(hn):
        acc = lax.dot_general(wv_ref[...], hn, (((1,), (1,)), ((), ())),
                              preferred_element_type=F32)
        ones = jnp.ones((VT_PAD, VT_CHUNK), BF16)
        for hh in range(acc.shape[0] // HEAD_DIM):
            for c in range(acc.shape[1] // VT_CHUNK):
                blk = acc[hh * HEAD_DIM:(hh + 1) * HEAD_DIM, c * VT_CHUNK:(c + 1) * VT_CHUNK]
                vt_ref[hh, c, 0:HEAD_DIM, :] = blk.astype(BF16)
                vt_ref[hh, c, HEAD_DIM:, :] = ones

    def conv_part():
        hn_all = hn_ref[...]
        hn = hn_ref[halo:halo + tm, :]
        cu = (jnp.dot(hn_all, wc_ref[...], preferred_element_type=F32)
              * jnp.dot(hn_all, wu_ref[...], preferred_element_type=F32))
        gate_b = jnp.dot(hn, wb_ref[...], preferred_element_type=F32)
        rows = cu.shape[0]
        before = pltpu.roll(cu, 1, 0)[halo:halo + tm]
        after = pltpu.roll(cu, rows - 1, 0)[halo:halo + tm]
        w = cw_ref[...]
        y = gate_b * (before * w[0:1, :] + cu[halo:halo + tm] * w[1:2, :] + after * w[2:3, :])
        cgain = cg_ref[...]
        for gi in range(y.shape[-1] // HEAD_DIM):
            sl = slice(gi * HEAD_DIM, (gi + 1) * HEAD_DIM)
            conv_ref[:, sl] = _rms(y[:, sl], cgain[:, sl]).astype(BF16)

    @pl.when(j == 0)
    def _():
        g = g_ref[...]
        first = i % tiles_per_seq == 0
        last = i % tiles_per_seq == tiles_per_seq - 1
        hn_ref[0:halo, :] = jnp.where(first, 0.0, _rms(xp_ref[...], g)).astype(BF16)
        hn_ref[halo + tm:, :] = jnp.where(last, 0.0, _rms(xn_ref[...], g)).astype(BF16)
        for r0, nr in _chunks(tm):
            hn = _rms(x_ref[r0:r0 + nr, :], g).astype(BF16)
            hn_ref[halo + r0:halo + r0 + nr, :] = hn
            head_norm(wq_ref, qg_ref, q_ref, hn, r0, nr)
            head_norm(wk_ref, kg_ref, k_ref, hn, r0, nr)
        conv_part()
        v_part(hn_ref[halo:halo + tm, :])

    @pl.when(j > 0)
    def _():
        conv_part()
        hn = hn_ref[halo:halo + tm, :]
        head_norm(wq_ref, qg_ref, q_ref, hn, 0, tm)
        head_norm(wk_ref, kg_ref, k_ref, hn, 0, tm)
        v_part(hn)


def _in_projection(x, norm1, w_in, w_vt, q_gain, k_gain, conv_w, conv_gain, layer, *,
                   seq, tm, tn):
    m, d = x.shape
    attn_w = N_HEADS * HEAD_DIM
    conv_width = conv_w.shape[-1]
    assert w_in.shape[-1] == 3 * attn_w + 3 * conv_width and attn_w == conv_width
    assert conv_width // CONV_GROUPS == HEAD_DIM and attn_w % tn == 0 and tn % HEAD_DIM == 0
    assert seq % tm == 0 and m % seq == 0
    nj = attn_w // tn
    halo = BF16_ROWS
    per = tm // halo
    n_halo = m // halo
    wspec = lambda part: pl.BlockSpec((None, d, tn), lambda i, j: (layer, 0, part * nj + j))
    gainspec = pl.BlockSpec((None, 1, HEAD_DIM), lambda i, j: (layer, 0, 0))
    headspec = pl.BlockSpec((tn // HEAD_DIM, tm, HEAD_DIM), lambda i, j: (j, i, 0))
    heads = jax.ShapeDtypeStruct((N_HEADS, m, HEAD_DIM), BF16)
    vt_rows = HEAD_DIM + VT_PAD
    vt_shape = jax.ShapeDtypeStruct((N_HEADS, m // VT_CHUNK, vt_rows, VT_CHUNK), BF16)
    vt_spec = pl.BlockSpec((tn // HEAD_DIM, tm // VT_CHUNK, vt_rows, VT_CHUNK),
                           lambda i, j: (j, i, 0, 0))
    return pl.pallas_call(
        functools.partial(_inproj_kernel, tiles_per_seq=seq // tm),
        out_shape=(heads, heads, vt_shape, jax.ShapeDtypeStruct((m, conv_width), BF16)),
        grid=(m // tm, nj),
        in_specs=[
            pl.BlockSpec((tm, d), lambda i, j: (i, 0)),
            pl.BlockSpec((halo, d), lambda i, j: (jnp.maximum(i * per - 1, 0), 0)),
            pl.BlockSpec((halo, d), lambda i, j: (jnp.minimum((i + 1) * per, n_halo - 1), 0)),
            pl.BlockSpec((None, 1, d), lambda i, j: (layer, 0, 0)),
            wspec(0), wspec(1),
            pl.BlockSpec((None, tn, d), lambda i, j: (layer, j, 0)),
            wspec(3), wspec(4), wspec(5),
            gainspec, gainspec,
            pl.BlockSpec((None, CONV_K, tn), lambda i, j: (layer, 0, j)),
            pl.BlockSpec((None, 1, tn), lambda i, j: (layer, 0, j)),
        ],
        out_specs=(headspec, headspec, vt_spec, pl.BlockSpec((tm, tn), lambda i, j: (i, j))),
        scratch_shapes=[pltpu.VMEM((tm + 2 * halo, d), BF16)],
        compiler_params=_params("parallel", "arbitrary"),
        name="in_projection",
    )(x, x, x, norm1, w_in, w_in, w_vt, w_in, w_in, w_in, q_gain, k_gain, conv_w, conv_gain)


def _row_group_variants(n_rows):
    variants = []
    for kind in range(3):
        table = []
        for i in range(ROW_GROUP):
            row = []
            for j in range(KEY_ROWS):
                if kind == 0:
                    q_abs, k_abs = i, j
                elif kind == 1:
                    q_abs, k_abs = KEY_ROWS + i, KEY_ROWS - WIN_ROWS // 2 + j
                else:
                    q_abs, k_abs = n_rows - ROW_GROUP + i, n_rows - KEY_ROWS + j
                r0 = q_abs - WIN_ROWS // 2
                if kind != 1:
                    r0 = min(max(r0, 0), n_rows - WIN_ROWS)
                valid = r0 <= k_abs < r0 + WIN_ROWS
                row.append(k_abs - q_abs + WIN_ROWS - 1 if valid else None)
            table.append(row)
        variants.append(table)
    return variants


def _attn_kernel(q_ref, k_ref, vt_ref, t_ref, g_ref, o_ref, bias_ref, s_ref, acc_ref, *, n_rows):
    n_groups = n_rows // ROW_GROUP
    gq = ROW_GROUP * GRID_W
    gk = KEY_ROWS * GRID_W
    rows_per_chunk = VT_CHUNK // GRID_W

    left = lax.broadcasted_iota(jnp.int32, (GRID_W, 2 * GRID_W), 1) < GRID_W
    neg = jnp.full((GRID_W, 2 * GRID_W), NEG_INF, F32)
    for kind, table in enumerate(_row_group_variants(n_rows)):
        for j in range(KEY_ROWS):
            for ii in range(ROW_GROUP // 2):
                dl, dr = table[2 * ii][j], table[2 * ii + 1][j]
                if dl is None and dr is None:
                    blk = neg
                else:
                    lo = neg if dl is None else t_ref[dl]
                    hi = neg if dr is None else t_ref[dr]
                    blk = jnp.where(left, lo, hi)
                bias_ref[kind, j * GRID_W:(j + 1) * GRID_W,
                         ii * 2 * GRID_W:(ii + 1) * 2 * GRID_W] = blk

    gain = g_ref[...]
    variants = _row_group_variants(n_rows)
    windows = []
    for ii in range(ROW_GROUP // 2):
        seen = [j for j in range(KEY_ROWS)
                if variants[1][2 * ii][j] is not None or variants[1][2 * ii + 1][j] is not None]
        windows.append((min(seen), max(seen) + 1))

    def key_row0(g):
        return jnp.clip(g * ROW_GROUP - WIN_ROWS // 2, 0, n_rows - KEY_ROWS)

    def logits(g):
        q = q_ref[pl.ds(pl.multiple_of(g * gq, gq), gq), :]
        k0 = pl.multiple_of(key_row0(g) * GRID_W, VT_CHUNK)
        return jnp.concatenate([
            lax.dot_general(k_ref[pl.ds(k0 + c * VT_CHUNK, VT_CHUNK), :], q,
                            (((1,), (1,)), ((), ())), preferred_element_type=F32)
            for c in range(gk // VT_CHUNK)], axis=0)

    def weights_interior(s):
        cols = []
        for ii, (lo, hi) in enumerate(windows):
            lanes = slice(ii * 2 * GRID_W, (ii + 1) * 2 * GRID_W)
            rows = slice(lo * GRID_W, hi * GRID_W)
            sc = s[rows, lanes] + bias_ref[1, rows, lanes]
            m = jnp.max(sc, axis=0, keepdims=True)
            pieces = [jnp.exp2(sc - m).astype(BF16)]
            if lo:
                pieces.insert(0, jnp.zeros((lo * GRID_W, 2 * GRID_W), BF16))
            if hi < KEY_ROWS:
                pieces.append(jnp.zeros(((KEY_ROWS - hi) * GRID_W, 2 * GRID_W), BF16))
            cols.append(jnp.concatenate(pieces, axis=0))
        return jnp.concatenate(cols, axis=1)

    def weights_edge(s, kind):
        s = s + bias_ref[kind]
        m = jnp.max(s, axis=0, keepdims=True)
        return jnp.exp2(s - m).astype(BF16)

    def attend(p, g):
        c0 = key_row0(g) // rows_per_chunk
        vt = jnp.concatenate([vt_ref[c0 + c] for c in range(gk // VT_CHUNK)], axis=1)
        o = jnp.dot(vt, p, preferred_element_type=F32)
        return o[0:HEAD_DIM] / o[HEAD_DIM:HEAD_DIM + 1]

    def emit(o, g):
        ms = jnp.mean(o * o, axis=0, keepdims=True)
        y = (o * lax.rsqrt(ms + EPS)).T * gain
        o_ref[pl.ds(pl.multiple_of(g * gq, gq), gq), :] = y.astype(o_ref.dtype)

    s_ref[...] = logits(0)
    acc_ref[...] = jnp.zeros_like(acc_ref)

    def body(t, carry):
        g0 = GROUPS_PER_ITER * t
        emit(acc_ref[...], jnp.maximum(g0 - 1, 0))
        s_cur = s_ref[...]
        for u in range(GROUPS_PER_ITER):
            g = g0 + u
            s_next = logits(jnp.minimum(g + 1, n_groups - 1))
            o = attend(weights_interior(s_cur), g)
            if u < GROUPS_PER_ITER - 1:
                emit(o, g)
            else:
                acc_ref[...] = o
            s_cur = s_next
        s_ref[...] = s_cur
        return carry

    lax.fori_loop(0, n_groups // GROUPS_PER_ITER, body, 0)

    for g, kind in ((0, 0), (n_groups - 1, 2)):
        g = jnp.int32(g)
        emit(attend(weights_edge(logits(g), kind), g), g)


def _attention(q, k, vt, bias_blocks, out_gain, layer, *, n_rows):
    _, m, _ = q.shape
    seq = n_rows * GRID_W
    b = m // seq
    gq, gk = ROW_GROUP * GRID_W, KEY_ROWS * GRID_W
    assert b * seq == m and n_rows % (GROUPS_PER_ITER * ROW_GROUP) == 0
    assert n_rows >= KEY_ROWS + ROW_GROUP and ROW_GROUP % 2 == 0
    assert seq % VT_CHUNK == 0 and gk % VT_CHUNK == 0 and gq % VT_CHUNK == 0
    assert (WIN_ROWS // 2 * GRID_W) % VT_CHUNK == 0
    blk = pl.BlockSpec((None, seq, HEAD_DIM), lambda h, bi: (h, bi, 0))
    return pl.pallas_call(
        functools.partial(_attn_kernel, n_rows=n_rows),
        out_shape=jax.ShapeDtypeStruct((N_HEADS, m, HEAD_DIM), BF16),
        grid=(N_HEADS, b),
        in_specs=[
            blk, blk,
            pl.BlockSpec((None, seq // VT_CHUNK, HEAD_DIM + VT_PAD, VT_CHUNK), lambda h, bi: (h, bi, 0, 0)),
            pl.BlockSpec((None, N_REL_ROWS, GRID_W, 2 * GRID_W), lambda h, bi: (h, 0, 0, 0)),
            pl.BlockSpec((None, 1, HEAD_DIM), lambda h, bi: (layer, 0, h)),
        ],
        out_specs=blk,
        scratch_shapes=[pltpu.VMEM((3, gk, gq), F32), pltpu.VMEM((gk, gq), F32),
                        pltpu.VMEM((HEAD_DIM, gq), F32)],
        compiler_params=_params("parallel", "parallel"),
        name="neighbourhood_attention",
    )(q, k, vt, bias_blocks, out_gain)


def _bias_blocks(rpb_layer):
    kc = np.arange(GRID_W)[:, None]
    c = np.arange(GRID_W)[None, :]
    start = np.clip(c - WIN_COLS // 2, 0, GRID_W - WIN_COLS)
    valid = (kc >= start) & (kc < start + WIN_COLS)
    select = (np.arange(N_REL_COLS)[:, None, None] == (kc - c + WIN_COLS - 1)[None]) & valid[None]
    t = jnp.einsum("hrd,dkc->hrkc", rpb_layer.astype(F32), jnp.asarray(select, F32),
                   precision=lax.Precision.HIGHEST)
    t = jnp.where(jnp.asarray(valid), t * LOG2_E, NEG_INF)
    return jnp.concatenate([t, t], axis=-1)


def _outproj_kernel(x_ref, a_ref, c_ref, w_ref, o_ref):
    lhs = jnp.concatenate([a_ref[h] for h in range(a_ref.shape[0])] + [c_ref[...]], axis=-1)
    o_ref[...] = x_ref[...] + jnp.dot(lhs, w_ref[...], preferred_element_type=F32)


def _out_projection(x, attn, conv, w_out, layer, *, tm):
    m, d = x.shape
    n_heads, _, head_dim = attn.shape
    aw, cw = n_heads * head_dim, conv.shape[-1]
    assert aw == cw and aw + cw == w_out.shape[1] and m % tm == 0
    return pl.pallas_call(
        _outproj_kernel,
        out_shape=jax.ShapeDtypeStruct((m, d), F32),
        grid=(m // tm,),
        in_specs=[
            pl.BlockSpec((tm, d), lambda i: (i, 0)),
            pl.BlockSpec((n_heads, tm, head_dim), lambda i: (0, i, 0)),
            pl.BlockSpec((tm, cw), lambda i: (i, 0)),
            pl.BlockSpec((None, aw + cw, d), lambda i: (layer, 0, 0)),
        ],
        out_specs=pl.BlockSpec((tm, d), lambda i: (i, 0)),
        compiler_params=_params("parallel"),
        name="out_projection",
    )(x, attn, conv, w_out)


def _ffn_kernel(x_ref, g_ref, wg_ref, wu_ref, wd_ref, o_ref, hn_ref):
    j = pl.program_id(1)
    tm = x_ref.shape[0]

    def swiglu(hn):
        gate = jnp.dot(hn, wg_ref[...], preferred_element_type=F32)
        up = jnp.dot(hn, wu_ref[...], preferred_element_type=F32)
        h = (gate * jax.nn.sigmoid(gate)) * up
        return jnp.dot(h.astype(BF16), wd_ref[...], preferred_element_type=F32)

    @pl.when(j == 0)
    def _():
        g = g_ref[...]
        for r0, nr in _chunks(tm):
            rows = slice(r0, r0 + nr)
            x = x_ref[rows, :]
            hn = _rms(x, g).astype(BF16)
            hn_ref[rows, :] = hn
            o_ref[rows, :] = x + swiglu(hn)

    @pl.when(j > 0)
    def _():
        o_ref[...] += swiglu(hn_ref[...])


def _feed_forward(x, norm2, w_gate, w_up, w_down, layer, *, tm, tf):
    m, d = x.shape
    f = w_gate.shape[-1]
    assert m % tm == 0 and f % tf == 0
    return pl.pallas_call(
        _ffn_kernel,
        out_shape=jax.ShapeDtypeStruct((m, d), F32),
        grid=(m // tm, f // tf),
        in_specs=[
            pl.BlockSpec((tm, d), lambda i, j: (i, 0)),
            pl.BlockSpec((None, 1, d), lambda i, j: (layer, 0, 0)),
            pl.BlockSpec((None, d, tf), lambda i, j: (layer, 0, j)),
            pl.BlockSpec((None, d, tf), lambda i, j: (layer, 0, j)),
            pl.BlockSpec((None, tf, d), lambda i, j: (layer, j, 0)),
        ],
        out_specs=pl.BlockSpec((tm, d), lambda i, j: (i, 0)),
        scratch_shapes=[pltpu.VMEM((tm, d), BF16)],
        compiler_params=_params("parallel", "arbitrary"),
        name="feed_forward",
    )(x, norm2, w_gate, w_up, w_down)


TILES = dict(
    inproj=dict(tm=1024, tn=512),
    outproj=dict(tm=1024),
    ffn=dict(tm=1024, tf=512),
)


def kernel(x_prompt, x_sample, norm1, w_in, q_gain, k_gain, rpb, conv_w, attn_out_gain,
           conv_out_gain, w_out, norm2, w_gate, w_up, w_down):
    depth, d, _ = w_in.shape
    w_in_b, w_out_b, w_gate_b, w_up_b, w_down_b = (
        w.astype(BF16) for w in (w_in, w_out, w_gate, w_up, w_down))
    attn_w = N_HEADS * HEAD_DIM
    w_vt_b = jnp.swapaxes(w_in_b[:, :, 2 * attn_w:3 * attn_w], 1, 2)
    row = lambda a: a.reshape(depth, 1, a.shape[-1])
    norm1_r, norm2_r = row(norm1), row(norm2)
    ag_r, cg_r = row(attn_out_gain), row(conv_out_gain)
    qg_r = row(q_gain.astype(F32) * (HEAD_DIM ** -0.5 * LOG2_E))
    kg_r = row(k_gain.astype(F32))

    streams = []
    for x in (x_prompt, x_sample):
        b, seq, _ = x.shape
        streams.append([x.reshape(b * seq, d), b, seq])

    for layer in range(depth):
        bias_blocks = _bias_blocks(rpb[layer])
        for st in streams:
            x, b, seq = st
            q, k, vt, conv = _in_projection(x, norm1_r, w_in_b, w_vt_b, qg_r, kg_r, conv_w, cg_r,
                                            layer, seq=seq, **TILES["inproj"])
            attn = _attention(q, k, vt, bias_blocks, ag_r, layer, n_rows=seq // GRID_W)
            x = _out_projection(x, attn, conv, w_out_b, layer, **TILES["outproj"])
            st[0] = _feed_forward(x, norm2_r, w_gate_b, w_up_b, w_down_b, layer, **TILES["ffn"])

    return tuple(x.reshape(b, seq, d) for x, b, seq in streams)
```

```python
import functools
import math

import numpy as np
import jax
import jax.numpy as jnp
from jax import lax
from jax.experimental import pallas as pl
from jax.experimental.pallas import tpu as pltpu

GRID_W = 64
N_HEADS = 8
HEAD_DIM = 128
CONV_GROUPS = 8
CONV_K = 3
WIN_ROWS = 8
WIN_COLS = 16
EPS = 1e-6
NEG_INF = -1e30
LOG2_E = math.log2(math.e)

BF16_ROWS = 16
NORM_CHUNKS = (512, 512)
VT_CHUNK = 256
VT_PAD = BF16_ROWS
VMEM_LIMIT = 60 * 1024 * 1024

ROW_GROUP = 4
KEY_ROWS = ROW_GROUP + WIN_ROWS
GROUPS_PER_ITER = 32
N_REL_ROWS = 2 * WIN_ROWS - 1
N_REL_COLS = 2 * WIN_COLS - 1

F32 = jnp.float32
BF16 = jnp.bfloat16


def _params(*sem):
    return pltpu.CompilerParams(dimension_semantics=sem, vmem_limit_bytes=VMEM_LIMIT)


def _chunks(tm):
    assert sum(NORM_CHUNKS) == tm and all(n % BF16_ROWS == 0 for n in NORM_CHUNKS)
    starts = np.cumsum((0,) + NORM_CHUNKS[:-1])
    return [(int(r0), nr) for r0, nr in zip(starts, NORM_CHUNKS)]


def _rms(x, gain):
    ms = jnp.mean(x * x, axis=-1, keepdims=True)
    return (x * lax.rsqrt(ms + EPS)) * gain


def _inproj_kernel(x_ref, xp_ref, xn_ref, g_ref, wq_ref, wk_ref, wv_ref, wb_ref, wc_ref, wu_ref,
                   qg_ref, kg_ref, cw_ref, cg_ref, q_ref, k_ref, vt_ref, conv_ref, hn_ref,
                   *, tiles_per_seq):
    i = pl.program_id(0)
    j = pl.program_id(1)
    tm = x_ref.shape[0]
    halo = xp_ref.shape[0]

    def head_norm(w_ref, gain_ref, o_ref, hn, r0, nr):
        acc = jnp.dot(hn, w_ref[...], preferred_element_type=F32)
        gain = gain_ref[...]
        for hh in range(acc.shape[-1] // HEAD_DIM):
            a = acc[:, hh * HEAD_DIM:(hh + 1) * HEAD_DIM]
            o_ref[hh, r0:r0 + nr, :] = _rms(a, gain).astype(BF16)

    def v_part(hn):
        acc = lax.dot_general(wv_ref[...], hn, (((1,), (1,)), ((), ())),
                              preferred_element_type=F32)
        ones = jnp.ones((VT_PAD, VT_CHUNK), BF16)
        for hh in range(acc.shape[0] // HEAD_DIM):
            for c in range(acc.shape[1] // VT_CHUNK):
                blk = acc[hh * HEAD_DIM:(hh + 1) * HEAD_DIM, c * VT_CHUNK:(c + 1) * VT_CHUNK]
                vt_ref[hh, c, 0:HEAD_DIM, :] = blk.astype(BF16)
                vt_ref[hh, c, HEAD_DIM:, :] = ones

    def conv_part():
        hn_all = hn_ref[...]
        hn = hn_ref[halo:halo + tm, :]
        cu = (jnp.dot(hn_all, wc_ref[...], preferred_element_type=F32)
              * jnp.dot(hn_all, wu_ref[...], preferred_element_type=F32))
        gate_b = jnp.dot(hn, wb_ref[...], preferred_element_type=F32)
        rows = cu.shape[0]
        before = pltpu.roll(cu, 1, 0)[halo:halo + tm]
        after = pltpu.roll(cu, rows - 1, 0)[halo:halo + tm]
        w = cw_ref[...]
        y = gate_b * (before * w[0:1, :] + cu[halo:halo + tm] * w[1:2, :] + after * w[2:3, :])
        cgain = cg_ref[...]
        for gi in range(y.shape[-1] // HEAD_DIM):
            sl = slice(gi * HEAD_DIM, (gi + 1) * HEAD_DIM)
            conv_ref[:, sl] = _rms(y[:, sl], cgain[:, sl]).astype(BF16)

    @pl.when(j == 0)
    def _():
        g = g_ref[...]
        first = i % tiles_per_seq == 0
        last = i % tiles_per_seq == tiles_per_seq - 1
        hn_ref[0:halo, :] = jnp.where(first, 0.0, _rms(xp_ref[...], g)).astype(BF16)
        hn_ref[halo + tm:, :] = jnp.where(last, 0.0, _rms(xn_ref[...], g)).astype(BF16)
        for r0, nr in _chunks(tm):
            hn = _rms(x_ref[r0:r0 + nr, :], g).astype(BF16)
            hn_ref[halo + r0:halo + r0 + nr, :] = hn
            head_norm(wq_ref, qg_ref, q_ref, hn, r0, nr)
            head_norm(wk_ref, kg_ref, k_ref, hn, r0, nr)
        conv_part()
        v_part(hn_ref[halo:halo + tm, :])

    @pl.when(j > 0)
    def _():
        conv_part()
        hn = hn_ref[halo:halo + tm, :]
        head_norm(wq_ref, qg_ref, q_ref, hn, 0, tm)
        head_norm(wk_ref, kg_ref, k_ref, hn, 0, tm)
        v_part(hn)


def _in_projection(x, norm1, w_in, w_vt, q_gain, k_gain, conv_w, conv_gain, layer, *,
                   seq, tm, tn):
    m, d = x.shape
    attn_w = N_HEADS * HEAD_DIM
    conv_width = conv_w.shape[-1]
    assert w_in.shape[-1] == 3 * attn_w + 3 * conv_width and attn_w == conv_width
    assert conv_width // CONV_GROUPS == HEAD_DIM and attn_w % tn == 0 and tn % HEAD_DIM == 0
    assert seq % tm == 0 and m % seq == 0
    nj = attn_w // tn
    halo = BF16_ROWS
    per = tm // halo
    n_halo = m // halo
    wspec = lambda part: pl.BlockSpec((None, d, tn), lambda i, j: (layer, 0, part * nj + j))
    gainspec = pl.BlockSpec((None, 1, HEAD_DIM), lambda i, j: (layer, 0, 0))
    headspec = pl.BlockSpec((tn // HEAD_DIM, tm, HEAD_DIM), lambda i, j: (j, i, 0))
    heads = jax.ShapeDtypeStruct((N_HEADS, m, HEAD_DIM), BF16)
    vt_rows = HEAD_DIM + VT_PAD
    vt_shape = jax.ShapeDtypeStruct((N_HEADS, m // VT_CHUNK, vt_rows, VT_CHUNK), BF16)
    vt_spec = pl.BlockSpec((tn // HEAD_DIM, tm // VT_CHUNK, vt_rows, VT_CHUNK),
                           lambda i, j: (j, i, 0, 0))
    return pl.pallas_call(
        functools.partial(_inproj_kernel, tiles_per_seq=seq // tm),
        out_shape=(heads, heads, vt_shape, jax.ShapeDtypeStruct((m, conv_width), BF16)),
        grid=(m // tm, nj),
        in_specs=[
            pl.BlockSpec((tm, d), lambda i, j: (i, 0)),
            pl.BlockSpec((halo, d), lambda i, j: (jnp.maximum(i * per - 1, 0), 0)),
            pl.BlockSpec((halo, d), lambda i, j: (jnp.minimum((i + 1) * per, n_halo - 1), 0)),
            pl.BlockSpec((None, 1, d), lambda i, j: (layer, 0, 0)),
            wspec(0), wspec(1),
            pl.BlockSpec((None, tn, d), lambda i, j: (layer, j, 0)),
            wspec(3), wspec(4), wspec(5),
            gainspec, gainspec,
            pl.BlockSpec((None, CONV_K, tn), lambda i, j: (layer, 0, j)),
            pl.BlockSpec((None, 1, tn), lambda i, j: (layer, 0, j)),
        ],
        out_specs=(headspec, headspec, vt_spec, pl.BlockSpec((tm, tn), lambda i, j: (i, j))),
        scratch_shapes=[pltpu.VMEM((tm + 2 * halo, d), BF16)],
        compiler_params=_params("parallel", "arbitrary"),
        name="in_projection",
    )(x, x, x, norm1, w_in, w_in, w_vt, w_in, w_in, w_in, q_gain, k_gain, conv_w, conv_gain)


def _row_group_variants(n_rows):
    variants = []
    for kind in range(3):
        table = []
        for i in range(ROW_GROUP):
            row = []
            for j in range(KEY_ROWS):
                if kind == 0:
                    q_abs, k_abs = i, j
                elif kind == 1:
                    q_abs, k_abs = KEY_ROWS + i, KEY_ROWS - WIN_ROWS // 2 + j
                else:
                    q_abs, k_abs = n_rows - ROW_GROUP + i, n_rows - KEY_ROWS + j
                r0 = q_abs - WIN_ROWS // 2
                if kind != 1:
                    r0 = min(max(r0, 0), n_rows - WIN_ROWS)
                valid = r0 <= k_abs < r0 + WIN_ROWS
                row.append(k_abs - q_abs + WIN_ROWS - 1 if valid else None)
            table.append(row)
        variants.append(table)
    return variants


def _attn_kernel(q_ref, k_ref, vt_ref, t_ref, g_ref, o_ref, bias_ref, s_ref, acc_ref, *, n_rows):
    n_groups = n_rows // ROW_GROUP
    gq = ROW_GROUP * GRID_W
    gk = KEY_ROWS * GRID_W
    rows_per_chunk = VT_CHUNK // GRID_W

    left = lax.broadcasted_iota(jnp.int32, (GRID_W, 2 * GRID_W), 1) < GRID_W
    neg = jnp.full((GRID_W, 2 * GRID_W), NEG_INF, F32)
    for kind, table in enumerate(_row_group_variants(n_rows)):
        for j in range(KEY_ROWS):
            for ii in range(ROW_GROUP // 2):
                dl, dr = table[2 * ii][j], table[2 * ii + 1][j]
                if dl is None and dr is None:
                    blk = neg
                else:
                    lo = neg if dl is None else t_ref[dl]
                    hi = neg if dr is None else t_ref[dr]
                    blk = jnp.where(left, lo, hi)
                bias_ref[kind, j * GRID_W:(j + 1) * GRID_W,
                         ii * 2 * GRID_W:(ii + 1) * 2 * GRID_W] = blk

    gain = g_ref[...]
    variants = _row_group_variants(n_rows)
    windows = []
    for ii in range(ROW_GROUP // 2):
        seen = [j for j in range(KEY_ROWS)
                if variants[1][2 * ii][j] is not None or variants[1][2 * ii + 1][j] is not None]
        windows.append((min(seen), max(seen) + 1))

    def key_row0(g):
        return jnp.clip(g * ROW_GROUP - WIN_ROWS // 2, 0, n_rows - KEY_ROWS)

    def logits(g):
        q = q_ref[pl.ds(pl.multiple_of(g * gq, gq), gq), :]
        k0 = pl.multiple_of(key_row0(g) * GRID_W, VT_CHUNK)
        return jnp.concatenate([
            lax.dot_general(k_ref[pl.ds(k0 + c * VT_CHUNK, VT_CHUNK), :], q,
                            (((1,), (1,)), ((), ())), preferred_element_type=F32)
            for c in range(gk // VT_CHUNK)], axis=0)

    def weights_interior(s):
        cols = []
        for ii, (lo, hi) in enumerate(windows):
            lanes = slice(ii * 2 * GRID_W, (ii + 1) * 2 * GRID_W)
            rows = slice(lo * GRID_W, hi * GRID_W)
            sc = s[rows, lanes] + bias_ref[1, rows, lanes]
            m = jnp.max(sc, axis=0, keepdims=True)
            pieces = [jnp.exp2(sc - m).astype(BF16)]
            if lo:
                pieces.insert(0, jnp.zeros((lo * GRID_W, 2 * GRID_W), BF16))
            if hi < KEY_ROWS:
                pieces.append(jnp.zeros(((KEY_ROWS - hi) * GRID_W, 2 * GRID_W), BF16))
            cols.append(jnp.concatenate(pieces, axis=0))
        return jnp.concatenate(cols, axis=1)

    def weights_edge(s, kind):
        s = s + bias_ref[kind]
        m = jnp.max(s, axis=0, keepdims=True)
        return jnp.exp2(s - m).astype(BF16)

    def attend(p, g):
        c0 = key_row0(g) // rows_per_chunk
        vt = jnp.concatenate([vt_ref[c0 + c] for c in range(gk // VT_CHUNK)], axis=1)
        o = jnp.dot(vt, p, preferred_element_type=F32)
        return o[0:HEAD_DIM] / o[HEAD_DIM:HEAD_DIM + 1]

    def emit(o, g):
        ms = jnp.mean(o * o, axis=0, keepdims=True)
        y = (o * lax.rsqrt(ms + EPS)).T * gain
        o_ref[pl.ds(pl.multiple_of(g * gq, gq), gq), :] = y.astype(o_ref.dtype)

    s_ref[...] = logits(0)
    acc_ref[...] = jnp.zeros_like(acc_ref)

    def body(t, carry):
        g0 = GROUPS_PER_ITER * t
        emit(acc_ref[...], jnp.maximum(g0 - 1, 0))
        s_cur = s_ref[...]
        for u in range(GROUPS_PER_ITER):
            g = g0 + u
            s_next = logits(jnp.minimum(g + 1, n_groups - 1))
            o = attend(weights_interior(s_cur), g)
            if u < GROUPS_PER_ITER - 1:
                emit(o, g)
            else:
                acc_ref[...] = o
            s_cur = s_next
        s_ref[...] = s_cur
        return carry

    lax.fori_loop(0, n_groups // GROUPS_PER_ITER, body, 0)

    for g, kind in ((0, 0), (n_groups - 1, 2)):
        g = jnp.int32(g)
        emit(attend(weights_edge(logits(g), kind), g), g)


def _attention(q, k, vt, bias_blocks, out_gain, layer, *, n_rows):
    _, m, _ = q.shape
    seq = n_rows * GRID_W
    b = m // seq
    gq, gk = ROW_GROUP * GRID_W, KEY_ROWS * GRID_W
    assert b * seq == m and n_rows % (GROUPS_PER_ITER * ROW_GROUP) == 0
    assert n_rows >= KEY_ROWS + ROW_GROUP and ROW_GROUP % 2 == 0
    assert seq % VT_CHUNK == 0 and gk % VT_CHUNK == 0 and gq % VT_CHUNK == 0
    assert (WIN_ROWS // 2 * GRID_W) % VT_CHUNK == 0
    blk = pl.BlockSpec((None, seq, HEAD_DIM), lambda h, bi: (h, bi, 0))
    return pl.pallas_call(
        functools.partial(_attn_kernel, n_rows=n_rows),
        out_shape=jax.ShapeDtypeStruct((N_HEADS, m, HEAD_DIM), BF16),
        grid=(N_HEADS, b),
        in_specs=[
            blk, blk,
            pl.BlockSpec((None, seq // VT_CHUNK, HEAD_DIM + VT_PAD, VT_CHUNK), lambda h, bi: (h, bi, 0, 0)),
            pl.BlockSpec((None, N_REL_ROWS, GRID_W, 2 * GRID_W), lambda h, bi: (h, 0, 0, 0)),
            pl.BlockSpec((None, 1, HEAD_DIM), lambda h, bi: (layer, 0, h)),
        ],
        out_specs=blk,
        scratch_shapes=[pltpu.VMEM((3, gk, gq), F32), pltpu.VMEM((gk, gq), F32),
                        pltpu.VMEM((HEAD_DIM, gq), F32)],
        compiler_params=_params("parallel", "parallel"),
        name="neighbourhood_attention",
    )(q, k, vt, bias_blocks, out_gain)


def _bias_blocks(rpb_layer):
    kc = np.arange(GRID_W)[:, None]
    c = np.arange(GRID_W)[None, :]
    start = np.clip(c - WIN_COLS // 2, 0, GRID_W - WIN_COLS)
    valid = (kc >= start) & (kc < start + WIN_COLS)
    select = (np.arange(N_REL_COLS)[:, None, None] == (kc - c + WIN_COLS - 1)[None]) & valid[None]
    t = jnp.einsum("hrd,dkc->hrkc", rpb_layer.astype(F32), jnp.asarray(select, F32),
                   precision=lax.Precision.HIGHEST)
    t = jnp.where(jnp.asarray(valid), t * LOG2_E, NEG_INF)
    return jnp.concatenate([t, t], axis=-1)


def _outproj_kernel(x_ref, a_ref, c_ref, w_ref, o_ref):
    lhs = jnp.concatenate([a_ref[h] for h in range(a_ref.shape[0])] + [c_ref[...]], axis=-1)
    o_ref[...] = x_ref[...] + jnp.dot(lhs, w_ref[...], preferred_element_type=F32)


def _out_projection(x, attn, conv, w_out, layer, *, tm):
    m, d = x.shape
    n_heads, _, head_dim = attn.shape
    aw, cw = n_heads * head_dim, conv.shape[-1]
    assert aw == cw and aw + cw == w_out.shape[1] and m % tm == 0
    return pl.pallas_call(
        _outproj_kernel,
        out_shape=jax.ShapeDtypeStruct((m, d), F32),
        grid=(m // tm,),
        in_specs=[
            pl.BlockSpec((tm, d), lambda i: (i, 0)),
            pl.BlockSpec((n_heads, tm, head_dim), lambda i: (0, i, 0)),
            pl.BlockSpec((tm, cw), lambda i: (i, 0)),
            pl.BlockSpec((None, aw + cw, d), lambda i: (layer, 0, 0)),
        ],
        out_specs=pl.BlockSpec((tm, d), lambda i: (i, 0)),
        compiler_params=_params("parallel"),
        name="out_projection",
    )(x, attn, conv, w_out)


def _ffn_kernel(x_ref, g_ref, wg_ref, wu_ref, wd_ref, o_ref, hn_ref):
    j = pl.program_id(1)
    tm = x_ref.shape[0]

    def swiglu(hn):
        tf = wg_ref.shape[-1]
        gu = jnp.dot(hn, jnp.concatenate([wg_ref[...], wu_ref[...]], axis=1),
                     preferred_element_type=F32)
        gate, up = gu[:, :tf], gu[:, tf:]
        h = (gate * jax.nn.sigmoid(gate)) * up
        return jnp.dot(h.astype(BF16), wd_ref[...], preferred_element_type=F32)

    @pl.when(j == 0)
    def _():
        g = g_ref[...]
        for r0, nr in _chunks(tm):
            rows = slice(r0, r0 + nr)
            x = x_ref[rows, :]
            hn = _rms(x, g).astype(BF16)
            hn_ref[rows, :] = hn
            o_ref[rows, :] = x + swiglu(hn)

    @pl.when(j > 0)
    def _():
        o_ref[...] += swiglu(hn_ref[...])


def _feed_forward(x, norm2, w_gate, w_up, w_down, layer, *, tm, tf):
    m, d = x.shape
    f = w_gate.shape[-1]
    assert m % tm == 0 and f % tf == 0
    return pl.pallas_call(
        _ffn_kernel,
        out_shape=jax.ShapeDtypeStruct((m, d), F32),
        grid=(m // tm, f // tf),
        in_specs=[
            pl.BlockSpec((tm, d), lambda i, j: (i, 0)),
            pl.BlockSpec((None, 1, d), lambda i, j: (layer, 0, 0)),
            pl.BlockSpec((None, d, tf), lambda i, j: (layer, 0, j)),
            pl.BlockSpec((None, d, tf), lambda i, j: (layer, 0, j)),
            pl.BlockSpec((None, tf, d), lambda i, j: (layer, j, 0)),
        ],
        out_specs=pl.BlockSpec((tm, d), lambda i, j: (i, 0)),
        scratch_shapes=[pltpu.VMEM((tm, d), BF16)],
        compiler_params=_params("parallel", "arbitrary"),
        name="feed_forward",
    )(x, norm2, w_gate, w_up, w_down)


TILES = dict(
    inproj=dict(tm=1024, tn=512),
    outproj=dict(tm=1024),
    ffn=dict(tm=1024, tf=512),
)


def kernel(x_prompt, x_sample, norm1, w_in, q_gain, k_gain, rpb, conv_w, attn_out_gain,
           conv_out_gain, w_out, norm2, w_gate, w_up, w_down):
    depth, d, _ = w_in.shape
    w_in_b, w_out_b, w_gate_b, w_up_b, w_down_b = (
        w.astype(BF16) for w in (w_in, w_out, w_gate, w_up, w_down))
    attn_w = N_HEADS * HEAD_DIM
    w_vt_b = jnp.swapaxes(w_in_b[:, :, 2 * attn_w:3 * attn_w], 1, 2)
    row = lambda a: a.reshape(depth, 1, a.shape[-1])
    norm1_r, norm2_r = row(norm1), row(norm2)
    ag_r, cg_r = row(attn_out_gain), row(conv_out_gain)
    qg_r = row(q_gain.astype(F32) * (HEAD_DIM ** -0.5 * LOG2_E))
    kg_r = row(k_gain.astype(F32))

    streams = []
    for x in (x_prompt, x_sample):
        b, seq, _ = x.shape
        streams.append([x.reshape(b * seq, d), b, seq])

    for layer in range(depth):
        bias_blocks = _bias_blocks(rpb[layer])
        for st in streams:
            x, b, seq = st
            q, k, vt, conv = _in_projection(x, norm1_r, w_in_b, w_vt_b, qg_r, kg_r, conv_w, cg_r,
                                            layer, seq=seq, **TILES["inproj"])
            attn = _attention(q, k, vt, bias_blocks, ag_r, layer, n_rows=seq // GRID_W)
            x = _out_projection(x, attn, conv, w_out_b, layer, **TILES["outproj"])
            st[0] = _feed_forward(x, norm2_r, w_gate_b, w_up_b, w_down_b, layer, **TILES["ffn"])

    return tuple(x.reshape(b, seq, d) for x, b, seq in streams)
```
